```python
import jax, jax.numpy as jnp
from jax import lax
import numpy as np

D_MODEL = 1024
BATCH = 2
SEQ = 16384
DEPTH = 2

MIX_WIDTH = D_MODEL
CONV_WIDTH = MIX_WIDTH // 2
CONV_GROUPS = 4
CONV_WIN = 31
DN_WIDTH = MIX_WIDTH - CONV_WIDTH
DN_HEADS = 4
DN_HEAD_DIM = DN_WIDTH // DN_HEADS
SHORT_CONV = 4
CHUNK = 64
D_FF_DENSE = 2816
N_EXPERTS = 8
TOP_K = 2
D_FF_EXPERT = 3584
N_DENSE = (DEPTH + 1) // 2
N_MOE = DEPTH // 2
DEEPNORM_ALPHA = (2 * DEPTH) ** 0.25
DEEPNORM_BETA = (8 * DEPTH) ** -0.25
LN_EPS = 1e-5
RMS_EPS = 1e-6
L2_EPS = 1e-6
IN_SIZES = (CONV_WIDTH, CONV_WIDTH, DN_WIDTH, DN_WIDTH, DN_WIDTH, DN_WIDTH, DN_HEADS, DN_HEADS)
IN_COLS = sum(IN_SIZES)
IN_SPLITS = tuple(int(s) for s in np.cumsum(IN_SIZES)[:-1])

kernel_name = "hybrid_conformer_deltanet_deepnorm_moe"


def layer_norm(x, g, b):
    xf = x.astype(jnp.float32)
    mu = xf.mean(-1, keepdims=True)
    var = jnp.square(xf - mu).mean(-1, keepdims=True)
    return ((xf - mu) * lax.rsqrt(var + LN_EPS) * g + b).astype(x.dtype)


def causal_depthwise_conv(x, w):
    width = w.shape[0]
    return lax.conv_general_dilated(
        x, w[:, None, :].astype(x.dtype), (1,), ((width - 1, 0),),
        dimension_numbers=("NWC", "WIO", "NWC"), feature_group_count=x.shape[-1])


def conformer_conv(h_val, h_gate, dw_w, dw_b, ln_g, ln_b):
    h = h_val * jax.nn.sigmoid(h_gate)
    h = causal_depthwise_conv(h, dw_w) + dw_b
    h = layer_norm(h, ln_g, ln_b)
    return jax.nn.silu(h)


def l2norm(t):
    return t * lax.rsqrt(jnp.sum(t * t, -1, keepdims=True) + L2_EPS)


def chunk_gated_delta_rule(q, k, v, g, beta):
    B, S, H, Dk = q.shape
    Dv = v.shape[-1]
    N = S // CHUNK
    q = q * (Dk ** -0.5)
    to_chunks = lambda t: t.reshape(B, N, CHUNK, H, t.shape[-1]).transpose(0, 3, 1, 2, 4)
    q, k, v = to_chunks(q), to_chunks(k), to_chunks(v)
    g = g.reshape(B, N, CHUNK, H).transpose(0, 3, 1, 2)
    beta = beta.reshape(B, N, CHUNK, H).transpose(0, 3, 1, 2)
    g = jnp.cumsum(g, axis=-1)
    kb = k * beta[..., None]
    vb = v * beta[..., None]
    causal = jnp.tril(jnp.ones((CHUNK, CHUNK), bool))
    strict = jnp.tril(jnp.ones((CHUNK, CHUNK), bool), -1)
    decay = jnp.exp(jnp.where(causal, g[..., :, None] - g[..., None, :], -jnp.inf))
    lower = jnp.where(strict, jnp.einsum('bhncd,bhnjd->bhncj', kb, k) * decay, 0.0)
    eye = jnp.eye(CHUNK, dtype=jnp.float32)
    rhs = jnp.concatenate([vb, kb * jnp.exp(g)[..., None]], axis=-1)
    sol = lax.linalg.triangular_solve(eye + lower, rhs, left_side=True, lower=True, unit_diagonal=True)
    u, w = sol[..., :Dv], sol[..., Dv:]
    attn = jnp.einsum('bhncd,bhnjd->bhncj', q, k) * decay

    def step(state, inp):
        qc, kc, uc, wc, gc, ac = inp
        v_new = uc - jnp.einsum('bhcd,bhde->bhce', wc, state)
        o = jnp.einsum('bhcd,bhde->bhce', qc * jnp.exp(gc)[..., None], state) \
            + jnp.einsum('bhcj,bhje->bhce', ac, v_new)
        g_last = gc[..., -1]
        k_dec = kc * jnp.exp(g_last[..., None] - gc)[..., None]
        state = state * jnp.exp(g_last)[..., None, None] + jnp.einsum('bhcd,bhce->bhde', k_dec, v_new)
        return state, o

    lead = lambda t: jnp.moveaxis(t, 2, 0)
    s0 = jnp.zeros((B, H, Dk, Dv), jnp.float32)
    _, o = lax.scan(step, s0, (lead(q), lead(k), lead(u), lead(w), lead(g), lead(attn)))
    o = jnp.moveaxis(o, 0, 2)
    return o.transpose(0, 2, 3, 1, 4).reshape(B, S, H, Dv)


def gated_deltanet(q, k, v, z, b_logit, a_logit, sc_w, a_log, dt_bias, on_g):
    B, S, _ = q.shape
    qkv = jax.nn.silu(causal_depthwise_conv(jnp.concatenate([q, k, v], -1), sc_w))
    q, k, v = jnp.split(qkv, 3, axis=-1)
    heads = lambda t: t.reshape(B, S, DN_HEADS, DN_HEAD_DIM).astype(jnp.float32)
    q, k, v = l2norm(heads(q)), l2norm(heads(k)), heads(v)
    beta = jax.nn.sigmoid(b_logit.astype(jnp.float32))
    g = -jnp.exp(a_log.astype(jnp.float32)) * jax.nn.softplus(
        a_logit.astype(jnp.float32) + dt_bias.astype(jnp.float32))
    o = chunk_gated_delta_rule(q, k, v, g, beta)
    o = o * lax.rsqrt(jnp.mean(o * o, -1, keepdims=True) + RMS_EPS) * on_g.astype(jnp.float32)
    o = o * jax.nn.silu(heads(z))
    return o.reshape(B, S, DN_WIDTH).astype(z.dtype)


def hybrid_mixer(x, w_in, dw_w, dw_b, cln_g, cln_b, sc_w, a_log, dt_bias, on_g, w_out):
    h = x @ w_in
    c_val, c_gate, q, k, v, z, b_logit, a_logit = jnp.split(h, IN_SPLITS, axis=-1)
    y_conv = conformer_conv(c_val, c_gate, dw_w, dw_b, cln_g, cln_b)
    y_dn = gated_deltanet(q, k, v, z, b_logit, a_logit, sc_w, a_log, dt_bias, on_g)
    return jnp.concatenate([y_conv, y_dn], axis=-1) @ w_out


def swiglu(x, w_gate, w_up, w_down):
    return (jax.nn.silu(x @ w_gate) * (x @ w_up)) @ w_down


def moe_swiglu(x, router_w, w_gate, w_up, w_down):
    logits = (x @ router_w).astype(jnp.float32)
    top_v, top_i = lax.top_k(logits, TOP_K)
    gates = jax.nn.softmax(top_v, axis=-1)
    combine = jnp.sum(jax.nn.one_hot(top_i, N_EXPERTS, dtype=jnp.float32) * gates[..., None], axis=-2)
    y = jnp.zeros_like(x)
    for e in range(N_EXPERTS):
        y = y + combine[..., e:e + 1].astype(x.dtype) * swiglu(x, w_gate[e], w_up[e], w_down[e])
    return y


def setup_inputs(seed: int = 0) -> dict:
    key = jax.random.key(seed)
    ks = jax.random.split(key, 24)
    f32 = jnp.float32
    nrm = lambda k, shape, scale: jax.random.normal(k, shape, f32) * scale
    dt = jnp.exp(jax.random.uniform(ks[8], (DEPTH, DN_HEADS), f32, np.log(1e-3), np.log(1e-1)))
    return {
        "x": nrm(ks[0], (BATCH, SEQ, D_MODEL), 1.0),
        "w_in": nrm(ks[1], (DEPTH, D_MODEL, IN_COLS), D_MODEL ** -0.5),
        "conv_dw_w": nrm(ks[2], (DEPTH, CONV_WIN, CONV_WIDTH), CONV_WIN ** -0.5),
        "conv_dw_b": nrm(ks[3], (DEPTH, CONV_WIDTH), 0.02),
        "conv_ln_g": 1.0 + nrm(ks[4], (DEPTH, CONV_WIDTH), 0.02),
        "conv_ln_b": nrm(ks[5], (DEPTH, CONV_WIDTH), 0.02),
        "short_conv_w": nrm(ks[6], (DEPTH, SHORT_CONV, 3 * DN_WIDTH), SHORT_CONV ** -0.5),
        "a_log": jnp.log(jax.random.uniform(ks[7], (DEPTH, DN_HEADS), f32, 1.0, 16.0)),
        "dt_bias": dt + jnp.log(-jnp.expm1(-dt)),
        "out_norm_g": 1.0 + nrm(ks[9], (DEPTH, DN_HEAD_DIM), 0.02),
        "w_out": nrm(ks[10], (DEPTH, MIX_WIDTH, D_MODEL), MIX_WIDTH ** -0.5 * DEEPNORM_BETA),
        "ln_mix_g": 1.0 + nrm(ks[11], (DEPTH, D_MODEL), 0.02),
        "ln_mix_b": nrm(ks[12], (DEPTH, D_MODEL), 0.02),
        "ffn_w_gate": nrm(ks[13], (N_DENSE, D_MODEL, D_FF_DENSE), D_MODEL ** -0.5),
        "ffn_w_up": nrm(ks[14], (N_DENSE, D_MODEL, D_FF_DENSE), D_MODEL ** -0.5),
        "ffn_w_down": nrm(ks[15], (N_DENSE, D_FF_DENSE, D_MODEL), D_FF_DENSE ** -0.5 * DEEPNORM_BETA),
        "router_w": nrm(ks[16], (N_MOE, D_MODEL, N_EXPERTS), D_MODEL ** -0.5),
        "moe_w_gate": nrm(ks[17], (N_MOE, N_EXPERTS, D_MODEL, D_FF_EXPERT), D_MODEL ** -0.5),
        "moe_w_up": nrm(ks[18], (N_MOE, N_EXPERTS, D_MODEL, D_FF_EXPERT), D_MODEL ** -0.5),
        "moe_w_down": nrm(ks[19], (N_MOE, N_EXPERTS, D_FF_EXPERT, D_MODEL), D_FF_EXPERT ** -0.5 * DEEPNORM_BETA),
        "ln_ffn_g": 1.0 + nrm(ks[20], (DEPTH, D_MODEL), 0.02),
        "ln_ffn_b": nrm(ks[21], (DEPTH, D_MODEL), 0.02),
    }


def reference(x, w_in, conv_dw_w, conv_dw_b, conv_ln_g, conv_ln_b, short_conv_w, a_log, dt_bias,
              out_norm_g, w_out, ln_mix_g, ln_mix_b, ffn_w_gate, ffn_w_up, ffn_w_down, router_w,
              moe_w_gate, moe_w_up, moe_w_down, ln_ffn_g, ln_ffn_b):
    for layer in range(DEPTH):
        m = hybrid_mixer(x, w_in[layer], conv_dw_w[layer], conv_dw_b[layer], conv_ln_g[layer],
                         conv_ln_b[layer], short_conv_w[layer], a_log[layer], dt_bias[layer],
                         out_norm_g[layer], w_out[layer])
        x = layer_norm(DEEPNORM_ALPHA * x + m, ln_mix_g[layer], ln_mix_b[layer])
        j = layer // 2
        if layer % 2 == 0:
            f = swiglu(x, ffn_w_gate[j], ffn_w_up[j], ffn_w_down[j])
        else:
            f = moe_swiglu(x, router_w[j], moe_w_gate[j], moe_w_up[j], moe_w_down[j])
        x = layer_norm(DEEPNORM_ALPHA * x + f, ln_ffn_g[layer], ln_ffn_b[layer])
    return x
```

```python
import functools

import jax
import jax.numpy as jnp
from jax import lax
from jax.experimental import pallas as pl
from jax.experimental.pallas import tpu as pltpu

F32 = jnp.float32
BF16 = jnp.bfloat16
I32 = jnp.int32

LANES = 128
SUBLANES = 8
VMEM_LIMIT = 56 * 1024 * 1024

DEPTH = 2
CONV_WIN = 31
SHORT_CONV = 4
DN_HEADS = 4
HEAD_DIM = 128
CHUNK = 64
TOP_K = 2
DEEPNORM_ALPHA = (2 * DEPTH) ** 0.25
LN_EPS = 1e-5
RMS_EPS = 1e-6
L2_EPS = 1e-6


def _cparams(n_axes):
    return pltpu.CompilerParams(dimension_semantics=("arbitrary",) * n_axes,
                                vmem_limit_bytes=VMEM_LIMIT)


def _sigmoid(x):
    return 1.0 / (1.0 + jnp.exp(-x))


def _silu(x):
    return x * _sigmoid(x)


def _layer_norm(y, g, b):
    mu = jnp.mean(y, -1, keepdims=True)
    d = y - mu
    var = jnp.mean(d * d, -1, keepdims=True)
    return d * lax.rsqrt(var + LN_EPS) * g + b


def _dot(a, b):
    return jnp.dot(a, b, preferred_element_type=F32)


def _dot_nt(a, b):
    return lax.dot_general(a, b, (((1,), (1,)), ((), ())), preferred_element_type=F32)


def _dot_tn(a, b):
    return lax.dot_general(a, b, (((0,), (0,)), ((), ())), preferred_element_type=F32)


def _inproj_kernel(x_ref, w_ref, o_ref):
    o_ref[...] = _dot(x_ref[...].astype(BF16), w_ref[...])


def _in_proj(xt, w):
    T, D = xt.shape
    N = w.shape[1]
    tm = min(512, T)
    return pl.pallas_call(
        _inproj_kernel,
        grid=(T // tm,),
        in_specs=[pl.BlockSpec((tm, D), lambda i: (i, 0)),
                  pl.BlockSpec((D, N), lambda i: (0, 0))],
        out_specs=pl.BlockSpec((tm, N), lambda i: (i, 0)),
        out_shape=jax.ShapeDtypeStruct((T, N), F32),
        compiler_params=_cparams(1),
    )(xt, w)


_CONV_HALO = 32
_CONV_SUB = 64


def _convmix_kernel(val_ref, gate_ref, w_ref, b_ref, g_ref, bb_ref, o_ref, ext_ref, sh_ref, *, ts):
    C = val_ref.shape[1]

    @pl.when(pl.program_id(1) == 0)
    def _():
        ext_ref[0:_CONV_HALO, :] = jnp.zeros((_CONV_HALO, C), F32)

    ext_ref[_CONV_HALO:_CONV_HALO + ts, :] = val_ref[...] * _sigmoid(gate_ref[...])
    for r in range(1, SUBLANES):
        sh_ref[r - 1, :, :] = ext_ref[r:r + ts + 24, :]

    first = _CONV_HALO - (CONV_WIN - 1)

    def body(i, carry):
        base = pl.multiple_of(i * _CONV_SUB, _CONV_SUB)
        acc = jnp.zeros((_CONV_SUB, C), F32) + b_ref[...]
        for j in range(CONV_WIN):
            a, r = divmod(first + j, SUBLANES)
            if r == 0:
                win = ext_ref[pl.ds(base + SUBLANES * a, _CONV_SUB), :]
            else:
                win = sh_ref[r - 1, pl.ds(base + SUBLANES * a, _CONV_SUB), :]
            acc = acc + w_ref[j:j + 1, :] * win
        y = _layer_norm(acc, g_ref[...], bb_ref[...])
        o_ref[pl.ds(base, _CONV_SUB), :] = _silu(y).astype(o_ref.dtype)
        return carry

    lax.fori_loop(0, ts // _CONV_SUB, body, 0)
    ext_ref[0:_CONV_HALO, :] = ext_ref[ts:ts + _CONV_HALO, :]


def _conv_mixer(h, B, S, dw_w, dw_b, ln_g, ln_b):
    C = dw_w.shape[1]
    T = B * S
    ts = min(512, S)
    ns = S // ts
    w = jnp.pad(dw_w, ((0, 32 - CONV_WIN), (0, 0)))
    row = lambda b, s: b * ns + s
    vec = pl.BlockSpec((1, C), lambda b, s: (0, 0))
    return pl.pallas_call(
        functools.partial(_convmix_kernel, ts=ts),
        grid=(B, ns),
        in_specs=[pl.BlockSpec((ts, C), lambda b, s: (row(b, s), 0)),
                  pl.BlockSpec((ts, C), lambda b, s: (row(b, s), 1)),
                  pl.BlockSpec((32, C), lambda b, s: (0, 0)),
                  vec, vec, vec],
        out_specs=pl.BlockSpec((ts, C), lambda b, s: (row(b, s), 0)),
        out_shape=jax.ShapeDtypeStruct((T, C), BF16),
        scratch_shapes=[pltpu.VMEM((ts + _CONV_HALO, C), F32),
                        pltpu.VMEM((SUBLANES - 1, ts + 24, C), F32)],
        compiler_params=_cparams(2),
    )(h, h, w, dw_b.reshape(1, C), ln_g.reshape(1, C), ln_b.reshape(1, C))


_DN_HALO = 8


def _split3(a):
    hi = a.astype(BF16)
    r1 = a - hi.astype(F32)
    mid = r1.astype(BF16)
    lo = (r1 - mid.astype(F32)).astype(BF16)
    return hi, mid, lo


def _deltanet_kernel(q_ref, k_ref, v_ref, z_ref, gt_ref, scw_ref, alog_ref, dtb_ref, ong_ref, o_ref,
                     ext_ref, state_ref, *, L):
    W = q_ref.shape[1]
    Cn, Dh = CHUNK, HEAD_DIM

    @pl.when(pl.program_id(1) == 0)
    def _():
        ext_ref[0:_DN_HALO, :] = jnp.zeros((_DN_HALO, 3 * W), F32)
        state_ref[...] = jnp.zeros(state_ref.shape, F32)

    ext_ref[_DN_HALO:_DN_HALO + L, 0:W] = q_ref[...]
    ext_ref[_DN_HALO:_DN_HALO + L, W:2 * W] = k_ref[...]
    ext_ref[_DN_HALO:_DN_HALO + L, 2 * W:3 * W] = v_ref[...]

    i6 = lax.broadcasted_iota(I32, (Cn, Cn), 0)
    j6 = lax.broadcasted_iota(I32, (Cn, Cn), 1)
    causal = i6 >= j6
    strict = i6 > j6
    eye = jnp.where(i6 == j6, 1.0, 0.0).astype(F32)
    i7 = lax.broadcasted_iota(I32, (Cn, 2 * Cn), 0)
    j7 = lax.broadcasted_iota(I32, (Cn, 2 * Cn), 1)
    after = jnp.logical_or(i7 > j7, j7 >= Cn)
    i9 = lax.broadcasted_iota(I32, (Cn, 3 * Cn), 0)
    j9 = lax.broadcasted_iota(I32, (Cn, 3 * Cn), 1)
    tri3 = jnp.where((j9 & (Cn - 1)) <= i9, 1.0, 0.0).astype(BF16)

    def level_mask(s):
        same = ((i6 ^ j6) >> (s.bit_length())) == 0
        return jnp.logical_and(same, jnp.logical_and((i6 & s) != 0, (j6 & s) == 0))

    first = _DN_HALO - (SHORT_CONV - 1)

    def conv_act(r0, c0):
        acc = None
        for j in range(SHORT_CONV):
            t = scw_ref[j:j + 1, c0:c0 + Dh] * ext_ref[first + j + r0:first + j + r0 + Cn, c0:c0 + Dh]
            acc = t if acc is None else acc + t
        return _silu(acc)

    states = [state_ref[h] for h in range(DN_HEADS)]
    for c in range(L // Cn):
        r0 = c * Cn
        gt = gt_ref[r0:r0 + Cn, :]
        beta_all = _sigmoid(gt)
        sp_in = gt + dtb_ref[...]
        softplus = jnp.maximum(sp_in, 0.0) + jnp.log1p(jnp.exp(-jnp.abs(sp_in)))
        g_all = -jnp.exp(alog_ref[...]) * softplus
        for h in range(DN_HEADS):
            c0 = h * Dh
            qh = conv_act(r0, c0)
            kh = conv_act(r0, W + c0)
            vh = conv_act(r0, 2 * W + c0)
            qh = qh * lax.rsqrt(jnp.sum(qh * qh, -1, keepdims=True) + L2_EPS) * (Dh ** -0.5)
            kh = kh * lax.rsqrt(jnp.sum(kh * kh, -1, keepdims=True) + L2_EPS)
            beta_b = jnp.broadcast_to(beta_all[:, h:h + 1], (Cn, Dh))
            g_b = jnp.broadcast_to(g_all[:, DN_HEADS + h:DN_HEADS + h + 1], (Cn, 2 * Cn))

            hi, mid, lo = _split3(jnp.where(after, g_b, 0.0))
            dm_full = _dot(tri3, jnp.concatenate([hi, mid, lo], axis=0))
            decay = jnp.where(causal, jnp.exp(dm_full[:, :Cn]), 0.0)
            G = jnp.broadcast_to(dm_full[:, Cn:Cn + 1], (Cn, Dh))
            G_last = jnp.broadcast_to(G[Cn - 1:Cn, :], (Cn, Dh))
            exp_g = jnp.exp(G)
            k_fac = jnp.exp(G_last - G)
            s_dec = jnp.exp(G_last)

            kb = kh * beta_b
            vb = vh * beta_b
            kh_bf = kh.astype(BF16)
            lower = jnp.where(strict, _dot_nt(kb.astype(BF16), kh_bf) * decay, 0.0)
            attn = _dot_nt(qh.astype(BF16), kh_bf) * decay

            X = eye - jnp.where(level_mask(1), lower, 0.0)
            s = 2
            while s < Cn:
                Cs = jnp.where(level_mask(s), lower, 0.0).astype(BF16)
                Xb = X.astype(BF16)
                X = X - _dot(_dot(Xb, Cs).astype(BF16), Xb)
                s *= 2
            sol = _dot(X.astype(BF16), jnp.concatenate([vb, kb * exp_g], axis=1).astype(BF16))
            u, w = sol[:, :Dh], sol[:, Dh:]

            S = states[h]
            r = _dot(jnp.concatenate([w, qh * exp_g], axis=0).astype(BF16), S.astype(BF16))
            v_new = u - r[:Cn]
            vnb = v_new.astype(BF16)
            o = r[Cn:] + _dot(attn.astype(BF16), vnb)
            states[h] = S * jnp.concatenate([s_dec, s_dec], axis=0) + _dot_tn((kh * k_fac).astype(BF16), vnb)

            on = o * lax.rsqrt(jnp.mean(o * o, -1, keepdims=True) + RMS_EPS) * ong_ref[...]
            zh = z_ref[r0:r0 + Cn, c0:c0 + Dh]
            o_ref[r0:r0 + Cn, c0:c0 + Dh] = (on * _silu(zh)).astype(o_ref.dtype)

    for h in range(DN_HEADS):
        state_ref[h] = states[h]
    ext_ref[0:_DN_HALO, :] = ext_ref[L:L + _DN_HALO, :]


def _deltanet(h, B, S, col0, sc_w, a_log, dt_bias, on_g, L):
    W = DN_HEADS * HEAD_DIM
    T = B * S
    L = min(L, S)
    ns = S // L
    cb = col0 // W
    gb = (col0 + 4 * W) // LANES
    row = lambda b, s: b * ns + s
    scw = jnp.pad(sc_w, ((0, SUBLANES - SHORT_CONV), (0, 0)))
    lane_pad = lambda v: jnp.pad(v.reshape(1, -1), ((0, 0), (DN_HEADS, LANES - 2 * DN_HEADS)))
    vec = pl.BlockSpec((1, LANES), lambda b, s: (0, 0))
    col = lambda j: pl.BlockSpec((L, W), lambda b, s: (row(b, s), cb + j))
    return pl.pallas_call(
        functools.partial(_deltanet_kernel, L=L),
        grid=(B, ns),
        in_specs=[col(0), col(1), col(2), col(3),
                  pl.BlockSpec((L, LANES), lambda b, s: (row(b, s), gb)),
                  pl.BlockSpec((SUBLANES, 3 * W), lambda b, s: (0, 0)),
                  vec, vec, vec],
        out_specs=pl.BlockSpec((L, W), lambda b, s: (row(b, s), 0)),
        out_shape=jax.ShapeDtypeStruct((T, W), BF16),
        scratch_shapes=[pltpu.VMEM((L + _DN_HALO, 3 * W), F32),
                        pltpu.VMEM((DN_HEADS, HEAD_DIM, HEAD_DIM), F32)],
        compiler_params=_cparams(2),
    )(h, h, h, h, h, scw, lane_pad(a_log), lane_pad(dt_bias), on_g.reshape(1, HEAD_DIM))


def _outproj_kernel(x_ref, yc_ref, yd_ref, w_ref, g_ref, b_ref, *rest, n_experts):
    Cc = yc_ref.shape[1]
    m = _dot(yc_ref[...], w_ref[0:Cc, :]) + _dot(yd_ref[...], w_ref[Cc:, :])
    x1 = _layer_norm(DEEPNORM_ALPHA * x_ref[...] + m, g_ref[...], b_ref[...])
    if not n_experts:
        (o_ref,) = rest
        o_ref[...] = x1
        return
    rw_ref, o_ref, meta_ref, gates_ref, cnt_ref, run_ref = rest
    o_ref[...] = x1
    tm = x1.shape[0]

    @pl.when(pl.program_id(0) == 0)
    def _():
        run_ref[...] = jnp.zeros(run_ref.shape, F32)

    logits = jnp.dot(x1, rw_ref[...], preferred_element_type=F32, precision=lax.Precision.HIGHEST)
    lane = lax.broadcasted_iota(I32, (tm, LANES), 1)
    lg = jnp.where(lane < n_experts, logits, -jnp.inf)
    v1 = jnp.max(lg, -1, keepdims=True)
    i1 = jnp.min(jnp.where(lg == v1, lane, LANES), -1, keepdims=True)
    lg2 = jnp.where(lane == i1, -jnp.inf, lg)
    v2 = jnp.max(lg2, -1, keepdims=True)
    i2 = jnp.min(jnp.where(lg2 == v2, lane, LANES), -1, keepdims=True)
    e2 = jnp.exp(v2 - v1)
    g1 = 1.0 / (1.0 + e2)
    g2 = e2 / (1.0 + e2)

    onehot = jnp.where(jnp.logical_or(lane == i1, lane == i2), 1.0, 0.0)
    ti = lax.broadcasted_iota(I32, (tm, tm), 0)
    tj = lax.broadcasted_iota(I32, (tm, tm), 1)
    before = jnp.where(tj < ti, 1.0, 0.0).astype(BF16)
    seen = _dot(before, onehot.astype(BF16)) + run_ref[0:1, :]
    r1 = jnp.sum(jnp.where(lane == i1, seen, 0.0), -1, keepdims=True).astype(I32)
    r2 = jnp.sum(jnp.where(lane == i2, seen, 0.0), -1, keepdims=True).astype(I32)
    run_ref[...] = run_ref[...] + jnp.sum(onehot, 0, keepdims=True)
    cnt_ref[...] = run_ref[...]

    meta_ref[...] = jnp.where(lane == 0, i1, jnp.where(lane == 1, i2, jnp.where(lane == 2, r1, jnp.where(lane == 3, r2, 0))))
    gates_ref[...] = jnp.where(lane == 0, g1, jnp.where(lane == 1, g2, 0.0))


def _out_proj(xt, yc, yd, w, g, b, router_w=None):
    T, D = xt.shape
    Cc, Cd = yc.shape[1], yd.shape[1]
    tm = min(512, T)
    vec = pl.BlockSpec((1, D), lambda i: (0, 0))
    rows = lambda n: pl.BlockSpec((tm, n), lambda i: (i, 0))
    in_specs = [rows(D), rows(Cc), rows(Cd), pl.BlockSpec((Cc + Cd, D), lambda i: (0, 0)), vec, vec]
    args = [xt, yc, yd, w, g.reshape(1, D), b.reshape(1, D)]
    if router_w is None:
        return pl.pallas_call(
            functools.partial(_outproj_kernel, n_experts=0),
            grid=(T // tm,), in_specs=in_specs, out_specs=rows(D),
            out_shape=jax.ShapeDtypeStruct((T, D), F32), compiler_params=_cparams(1),
        )(*args)
    E = router_w.shape[1]
    rw = jnp.pad(router_w, ((0, 0), (0, LANES - E)))
    return pl.pallas_call(
        functools.partial(_outproj_kernel, n_experts=E),
        grid=(T // tm,),
        in_specs=in_specs + [pl.BlockSpec((D, LANES), lambda i: (0, 0))],
        out_specs=[rows(D), rows(LANES), rows(LANES), pl.BlockSpec((SUBLANES, LANES), lambda i: (0, 0))],
        out_shape=[jax.ShapeDtypeStruct((T, D), F32), jax.ShapeDtypeStruct((T, LANES), I32),
                   jax.ShapeDtypeStruct((T, LANES), F32), jax.ShapeDtypeStruct((SUBLANES, LANES), F32)],
        scratch_shapes=[pltpu.VMEM((SUBLANES, LANES), F32)],
        compiler_params=_cparams(1),
    )(*args, rw)


def _swiglu_step(xb, wg_ref, wu_ref, wd_ref, acc_ref, f):
    gate = _dot(xb, wg_ref[...])
    up = _dot(xb, wu_ref[...])
    part = _dot((_silu(gate) * up).astype(BF16), wd_ref[...])

    @pl.when(f == 0)
    def _():
        acc_ref[...] = part

    @pl.when(f > 0)
    def _():
        acc_ref[...] = acc_ref[...] + part


def _ffn_dense_kernel(x_ref, wg_ref, wu_ref, wd_ref, g_ref, b_ref, o_ref, xb_ref, acc_ref):
    f = pl.program_id(1)

    @pl.when(f == 0)
    def _():
        xb_ref[...] = x_ref[...].astype(BF16)

    _swiglu_step(xb_ref[...], wg_ref, wu_ref, wd_ref, acc_ref, f)

    @pl.when(f == pl.num_programs(1) - 1)
    def _():
        o_ref[...] = _layer_norm(DEEPNORM_ALPHA * x_ref[...] + acc_ref[...], g_ref[...], b_ref[...])


def _ff_tile(F, target):
    best = LANES
    for t in range(LANES, min(F, target) + 1, LANES):
        if F % t == 0:
            best = t
    return best


def _ffn_dense(xt, wg, wu, wd, g, b):
    T, D = xt.shape
    F = wg.shape[1]
    tm = min(512, T)
    tf = _ff_tile(F, 1408)
    vec = pl.BlockSpec((1, D), lambda i, f: (0, 0))
    return pl.pallas_call(
        _ffn_dense_kernel,
        grid=(T // tm, F // tf),
        in_specs=[pl.BlockSpec((tm, D), lambda i, f: (i, 0)),
                  pl.BlockSpec((D, tf), lambda i, f: (0, f)),
                  pl.BlockSpec((D, tf), lambda i, f: (0, f)),
                  pl.BlockSpec((tf, D), lambda i, f: (f, 0)),
                  vec, vec],
        out_specs=pl.BlockSpec((tm, D), lambda i, f: (i, 0)),
        out_shape=jax.ShapeDtypeStruct((T, D), F32),
        scratch_shapes=[pltpu.VMEM((tm, D), BF16), pltpu.VMEM((tm, D), F32)],
        compiler_params=_cparams(2),
    )(xt, wg, wu, wd, g.reshape(1, D), b.reshape(1, D))


def _moe_kernel(te_ref, nv_ref, src_ref, x_hbm, wg_ref, wu_ref, wd_ref, o_ref, xs_ref, xb_ref, acc_ref, sem):
    i = pl.program_id(0)
    f = pl.program_id(1)
    tm = xs_ref.shape[0]

    @pl.when(i < nv_ref[0])
    def _():
        @pl.when(f == 0)
        def _():
            def issue(r, carry):
                pltpu.make_async_copy(x_hbm.at[pl.ds(src_ref[0, 0, r], 1), :], xs_ref.at[pl.ds(r, 1), :], sem).start()
                return carry

            lax.fori_loop(0, tm, issue, 0)
            pltpu.make_async_copy(x_hbm.at[pl.ds(0, tm), :], xs_ref, sem).wait()
            xb_ref[...] = xs_ref[...].astype(BF16)

        _swiglu_step(xb_ref[...], wg_ref, wu_ref, wd_ref, acc_ref, f)

        @pl.when(f == pl.num_programs(1) - 1)
        def _():
            o_ref[...] = acc_ref[...]

    @pl.when(jnp.logical_and(i >= nv_ref[0], f == 0))
    def _():
        o_ref[...] = jnp.zeros(o_ref.shape, o_ref.dtype)


def _moe_experts(x1, src, tile_expert, n_valid, wg, wu, wd, tm):
    T, D = x1.shape
    F = wg.shape[2]
    n_tiles = src.shape[0]
    tf = _ff_tile(F, 896)
    grid_spec = pltpu.PrefetchScalarGridSpec(
        num_scalar_prefetch=2,
        grid=(n_tiles, F // tf),
        in_specs=[pl.BlockSpec((1, 1, tm), lambda i, f, te, nv: (i, 0, 0), memory_space=pltpu.SMEM),
                  pl.BlockSpec(memory_space=pl.ANY),
                  pl.BlockSpec((None, D, tf), lambda i, f, te, nv: (te[i], 0, f)),
                  pl.BlockSpec((None, D, tf), lambda i, f, te, nv: (te[i], 0, f)),
                  pl.BlockSpec((None, tf, D), lambda i, f, te, nv: (te[i], f, 0))],
        out_specs=pl.BlockSpec((tm, D), lambda i, f, te, nv: (i, 0)),
        scratch_shapes=[pltpu.VMEM((tm, D), F32), pltpu.VMEM((tm, D), BF16), pltpu.VMEM((tm, D), F32),
                        pltpu.SemaphoreType.DMA(())],
    )
    return pl.pallas_call(
        _moe_kernel, grid_spec=grid_spec,
        out_shape=jax.ShapeDtypeStruct((n_tiles * tm, D), F32),
        compiler_params=_cparams(2),
    )(tile_expert, n_valid, src, x1, wg, wu, wd)


def _combine_kernel(pos_ref, x_ref, gates_ref, ys_hbm, g_ref, b_ref, o_ref, a_ref, b2_ref, sem):
    tc = x_ref.shape[0]

    def issue(r, carry):
        pltpu.make_async_copy(ys_hbm.at[pl.ds(pos_ref[0, 0, 2 * r], 1), :], a_ref.at[pl.ds(r, 1), :], sem).start()
        pltpu.make_async_copy(ys_hbm.at[pl.ds(pos_ref[0, 0, 2 * r + 1], 1), :], b2_ref.at[pl.ds(r, 1), :], sem).start()
        return carry

    lax.fori_loop(0, tc, issue, 0)
    pltpu.make_async_copy(ys_hbm.at[pl.ds(0, tc), :], a_ref, sem).wait()
    pltpu.make_async_copy(ys_hbm.at[pl.ds(0, tc), :], b2_ref, sem).wait()
    gts = gates_ref[...]
    f = gts[:, 0:1] * a_ref[...] + gts[:, 1:2] * b2_ref[...]
    o_ref[...] = _layer_norm(DEEPNORM_ALPHA * x_ref[...] + f, g_ref[...], b_ref[...])


def _moe_combine(x1, pos, gates, ys, g, b):
    T, D = x1.shape
    tc = min(256, T)
    vec = pl.BlockSpec((1, D), lambda i: (0, 0))
    return pl.pallas_call(
        _combine_kernel,
        grid=(T // tc,),
        in_specs=[pl.BlockSpec((1, 1, TOP_K * tc), lambda i: (i, 0, 0), memory_space=pltpu.SMEM),
                  pl.BlockSpec((tc, D), lambda i: (i, 0)),
                  pl.BlockSpec((tc, LANES), lambda i: (i, 0)),
                  pl.BlockSpec(memory_space=pl.ANY),
                  vec, vec],
        out_specs=pl.BlockSpec((tc, D), lambda i: (i, 0)),
        out_shape=jax.ShapeDtypeStruct((T, D), F32),
        scratch_shapes=[pltpu.VMEM((tc, D), F32), pltpu.VMEM((tc, D), F32), pltpu.SemaphoreType.DMA(())],
        compiler_params=_cparams(1),
    )(pos.reshape(T // tc, 1, TOP_K * tc), x1, gates, ys, g.reshape(1, D), b.reshape(1, D))


def _moe_ffn(x1, meta, gates, counts, wg, wu, wd, g, b):
    T, D = x1.shape
    E = wg.shape[0]
    tm = min(512, T)
    n_tiles = (TOP_K * T) // tm + E - 1
    cnt = counts[0, :E].astype(I32)
    tiles_e = (cnt + tm - 1) // tm
    tile_end = jnp.cumsum(tiles_e)
    row_off = (tile_end - tiles_e) * tm
    idx, rank = meta[:, 0:TOP_K], meta[:, TOP_K:2 * TOP_K]
    pos = row_off[idx] + rank
    tok = jnp.broadcast_to(jnp.arange(T, dtype=I32)[:, None], (T, TOP_K))
    src = jnp.zeros((n_tiles * tm,), I32).at[pos.reshape(-1)].set(tok.reshape(-1), unique_indices=True)
    tile_expert = jnp.minimum(jnp.searchsorted(tile_end, jnp.arange(n_tiles, dtype=I32), side="right"), E - 1).astype(I32)
    n_valid = tile_end[E - 1:E].astype(I32)
    ys = _moe_experts(x1, src.reshape(n_tiles, 1, tm), tile_expert, n_valid, wg, wu, wd, tm)
    return _moe_combine(x1, pos, gates, ys, g, b)


def kernel(x, w_in, conv_dw_w, conv_dw_b, conv_ln_g, conv_ln_b, short_conv_w, a_log, dt_bias, out_norm_g, w_out, ln_mix_g, ln_mix_b, ffn_w_gate, ffn_w_up, ffn_w_down, router_w, moe_w_gate, moe_w_up, moe_w_down, ln_ffn_g, ln_ffn_b):
    B, S, D = x.shape
    T = B * S
    xt = x.reshape(T, D)
    Cc = conv_dw_w.shape[2]
    in_cols = w_in.shape[2]
    in_cols_pad = -(-in_cols // LANES) * LANES
    for layer in range(w_in.shape[0]):
        w_in_p = jnp.pad(w_in[layer], ((0, 0), (0, in_cols_pad - in_cols))).astype(BF16)
        h = _in_proj(xt, w_in_p)
        yc = _conv_mixer(h, B, S, conv_dw_w[layer], conv_dw_b[layer], conv_ln_g[layer], conv_ln_b[layer])
        yd = _deltanet(h, B, S, 2 * Cc, short_conv_w[layer], a_log[layer], dt_bias[layer], out_norm_g[layer], L=256)
        j = layer // 2
        w_o = w_out[layer].astype(BF16)
        if layer % 2 == 0:
            x1 = _out_proj(xt, yc, yd, w_o, ln_mix_g[layer], ln_mix_b[layer])
            xt = _ffn_dense(x1, ffn_w_gate[j].astype(BF16), ffn_w_up[j].astype(BF16), ffn_w_down[j].astype(BF16),
                            ln_ffn_g[layer], ln_ffn_b[layer])
        else:
            x1, meta, gates, counts = _out_proj(xt, yc, yd, w_o, ln_mix_g[layer], ln_mix_b[layer], router_w[j])
            xt = _moe_ffn(x1, meta, gates, counts, moe_w_gate[j].astype(BF16), moe_w_up[j].astype(BF16),
                          moe_w_down[j].astype(BF16), ln_ffn_g[layer], ln_ffn_b[layer])
    return xt.reshape(B, S, D)
```

```python
import functools

import jax
import jax.numpy as jnp
from jax import lax
from jax.experimental import pallas as pl
from jax.experimental.pallas import tpu as pltpu

F32 = jnp.float32
BF16 = jnp.bfloat16
I32 = jnp.int32

LANES = 128
SUBLANES = 8
VMEM_LIMIT = 56 * 1024 * 1024

DEPTH = 2
CONV_WIN = 31
SHORT_CONV = 4
DN_HEADS = 4
HEAD_DIM = 128
CHUNK = 64
TOP_K = 2
DEEPNORM_ALPHA = (2 * DEPTH) ** 0.25
LN_EPS = 1e-5
RMS_EPS = 1e-6
L2_EPS = 1e-6


def _cparams(n_axes):
    return pltpu.CompilerParams(dimension_semantics=("arbitrary",) * n_axes,
                                vmem_limit_bytes=VMEM_LIMIT)


def _resident(shape, index_map):
    return pl.BlockSpec(shape, index_map, pipeline_mode=pl.Buffered(1))


def _sigmoid(x):
    return 1.0 / (1.0 + jnp.exp(-x))


def _silu(x):
    return x * _sigmoid(x)


def _layer_norm(y, g, b):
    mu = jnp.mean(y, -1, keepdims=True)
    d = y - mu
    var = jnp.mean(d * d, -1, keepdims=True)
    return d * lax.rsqrt(var + LN_EPS) * g + b


def _dot(a, b):
    return jnp.dot(a, b, preferred_element_type=F32)


def _dot_nt(a, b):
    return lax.dot_general(a, b, (((1,), (1,)), ((), ())), preferred_element_type=F32)


def _dot_tn(a, b):
    return lax.dot_general(a, b, (((0,), (0,)), ((), ())), preferred_element_type=F32)


def _inproj_kernel(x_ref, w_ref, o_ref):
    o_ref[...] = _dot(x_ref[...].astype(BF16), w_ref[...])


def _in_proj(xt, w):
    T, D = xt.shape
    N = w.shape[1]
    tm = min(512, T)
    return pl.pallas_call(
        _inproj_kernel,
        grid=(T // tm,),
        in_specs=[pl.BlockSpec((tm, D), lambda i: (i, 0)),
                  _resident((D, N), lambda i: (0, 0))],
        out_specs=pl.BlockSpec((tm, N), lambda i: (i, 0)),
        out_shape=jax.ShapeDtypeStruct((T, N), F32),
        compiler_params=_cparams(1),
    )(xt, w)


_CONV_HALO = 32
_CONV_SUB = 64


def _convmix_kernel(val_ref, gate_ref, w_ref, b_ref, g_ref, bb_ref, o_ref, ext_ref, sh_ref, *, ts):
    C = val_ref.shape[1]

    @pl.when(pl.program_id(1) == 0)
    def _():
        ext_ref[0:_CONV_HALO, :] = jnp.zeros((_CONV_HALO, C), F32)

    ext_ref[_CONV_HALO:_CONV_HALO + ts, :] = val_ref[...] * _sigmoid(gate_ref[...])
    for r in range(1, SUBLANES):
        sh_ref[r - 1, :, :] = ext_ref[r:r + ts + 24, :]

    first = _CONV_HALO - (CONV_WIN - 1)

    def body(i, carry):
        base = pl.multiple_of(i * _CONV_SUB, _CONV_SUB)
        acc = jnp.zeros((_CONV_SUB, C), F32) + b_ref[...]
        for j in range(CONV_WIN):
            a, r = divmod(first + j, SUBLANES)
            if r == 0:
                win = ext_ref[pl.ds(base + SUBLANES * a, _CONV_SUB), :]
            else:
                win = sh_ref[r - 1, pl.ds(base + SUBLANES * a, _CONV_SUB), :]
            acc = acc + w_ref[j:j + 1, :] * win
        y = _layer_norm(acc, g_ref[...], bb_ref[...])
        o_ref[pl.ds(base, _CONV_SUB), :] = _silu(y).astype(o_ref.dtype)
        return carry

    lax.fori_loop(0, ts // _CONV_SUB, body, 0)
    ext_ref[0:_CONV_HALO, :] = ext_ref[ts:ts + _CONV_HALO, :]


def _conv_mixer(h, B, S, dw_w, dw_b, ln_g, ln_b):
    C = dw_w.shape[1]
    T = B * S
    ts = min(512, S)
    ns = S // ts
    w = jnp.pad(dw_w, ((0, 32 - CONV_WIN), (0, 0)))
    row = lambda b, s: b * ns + s
    vec = pl.BlockSpec((1, C), lambda b, s: (0, 0))
    return pl.pallas_call(
        functools.partial(_convmix_kernel, ts=ts),
        grid=(B, ns),
        in_specs=[pl.BlockSpec((ts, C), lambda b, s: (row(b, s), 0)),
                  pl.BlockSpec((ts, C), lambda b, s: (row(b, s), 1)),
                  pl.BlockSpec((32, C), lambda b, s: (0, 0)),
                  vec, vec, vec],
        out_specs=pl.BlockSpec((ts, C), lambda b, s: (row(b, s), 0)),
        out_shape=jax.ShapeDtypeStruct((T, C), BF16),
        scratch_shapes=[pltpu.VMEM((ts + _CONV_HALO, C), F32),
                        pltpu.VMEM((SUBLANES - 1, ts + 24, C), F32)],
        compiler_params=_cparams(2),
    )(h, h, w, dw_b.reshape(1, C), ln_g.reshape(1, C), ln_b.reshape(1, C))


_DN_HALO = 8


def _split3(a):
    hi = a.astype(BF16)
    r1 = a - hi.astype(F32)
    mid = r1.astype(BF16)
    lo = (r1 - mid.astype(F32)).astype(BF16)
    return hi, mid, lo


def _deltanet_kernel(q_ref, k_ref, v_ref, z_ref, gt_ref, scw_ref, alog_ref, dtb_ref, ong_ref, o_ref,
                     ext_ref, state_ref, *, L):
    nb, _, W = q_ref.shape
    Cn, Dh, H = CHUNK, HEAD_DIM, DN_HEADS
    nc = L // Cn

    @pl.when(pl.program_id(0) == 0)
    def _():
        ext_ref[:, 0:_DN_HALO, :] = jnp.zeros((nb, _DN_HALO, 3 * W), F32)
        state_ref[...] = jnp.zeros(state_ref.shape, F32)

    for b in range(nb):
        ext_ref[b, _DN_HALO:_DN_HALO + L, 0:W] = q_ref[b]
        ext_ref[b, _DN_HALO:_DN_HALO + L, W:2 * W] = k_ref[b]
        ext_ref[b, _DN_HALO:_DN_HALO + L, 2 * W:3 * W] = v_ref[b]

    i6 = lax.broadcasted_iota(I32, (Cn, Cn), 0)
    j6 = lax.broadcasted_iota(I32, (Cn, Cn), 1)
    causal = i6 >= j6
    strict = i6 > j6
    eye = jnp.where(i6 == j6, 1.0, 0.0).astype(F32)
    i7 = lax.broadcasted_iota(I32, (Cn, 2 * Cn), 0)
    j7 = lax.broadcasted_iota(I32, (Cn, 2 * Cn), 1)
    after = jnp.logical_or(i7 > j7, j7 >= Cn)
    i9 = lax.broadcasted_iota(I32, (Cn, 3 * Cn), 0)
    j9 = lax.broadcasted_iota(I32, (Cn, 3 * Cn), 1)
    tri3 = jnp.where((j9 & (Cn - 1)) <= i9, 1.0, 0.0).astype(BF16)

    def level_mask(s):
        same = ((i6 ^ j6) >> (s.bit_length())) == 0
        return jnp.logical_and(same, jnp.logical_and((i6 & s) != 0, (j6 & s) == 0))

    first = _DN_HALO - (SHORT_CONV - 1)

    def conv_act(b, r0, c0):
        acc = None
        for j in range(SHORT_CONV):
            t = scw_ref[j:j + 1, c0:c0 + Dh] * ext_ref[b, first + j + r0:first + j + r0 + Cn, c0:c0 + Dh]
            acc = t if acc is None else acc + t
        return _silu(acc)

    tiles = [(b, c) for b in range(nb) for c in range(nc)]
    units = [(b, c, h) for (b, c) in tiles for h in range(H)]

    beta_all, g_all = {}, {}
    for (b, c) in tiles:
        gt = gt_ref[b, c * Cn:(c + 1) * Cn, :]
        beta_all[b, c] = _sigmoid(gt)
        sp_in = gt + dtb_ref[...]
        softplus = jnp.maximum(sp_in, 0.0) + jnp.log1p(jnp.exp(-jnp.abs(sp_in)))
        g_all[b, c] = -jnp.exp(alog_ref[...]) * softplus

    dm = {}
    for (b, c, h) in units:
        g_b = jnp.broadcast_to(g_all[b, c][:, H + h:H + h + 1], (Cn, 2 * Cn))
        hi, mid, lo = _split3(jnp.where(after, g_b, 0.0))
        dm[b, c, h] = _dot(tri3, jnp.concatenate([hi, mid, lo], axis=0))

    qn, kn, kb, vb = {}, {}, {}, {}
    for (b, c, h) in units:
        r0, c0 = c * Cn, h * Dh
        qh = conv_act(b, r0, c0)
        kh = conv_act(b, r0, W + c0)
        vh = conv_act(b, r0, 2 * W + c0)
        qn[b, c, h] = qh * lax.rsqrt(jnp.sum(qh * qh, -1, keepdims=True) + L2_EPS) * (Dh ** -0.5)
        kh = kh * lax.rsqrt(jnp.sum(kh * kh, -1, keepdims=True) + L2_EPS)
        beta_b = jnp.broadcast_to(beta_all[b, c][:, h:h + 1], (Cn, Dh))
        kn[b, c, h] = kh
        kb[b, c, h] = kh * beta_b
        vb[b, c, h] = vh * beta_b

    decay, G = {}, {}
    for u in units:
        decay[u] = jnp.where(causal, jnp.exp(dm[u][:, :Cn]), 0.0)
        G[u] = jnp.broadcast_to(dm[u][:, Cn:Cn + 1], (Cn, Dh))

    lower, attn = {}, {}
    for u in units:
        kh_bf = kn[u].astype(BF16)
        lower[u] = jnp.where(strict, _dot_nt(kb[u].astype(BF16), kh_bf) * decay[u], 0.0)
        attn[u] = (_dot_nt(qn[u].astype(BF16), kh_bf) * decay[u]).astype(BF16)

    m1 = level_mask(1)
    X = {u: eye - jnp.where(m1, lower[u], 0.0) for u in units}
    s = 2
    while s < Cn:
        ms = level_mask(s)
        Xb = {u: X[u].astype(BF16) for u in units}
        XC = {u: _dot(Xb[u], jnp.where(ms, lower[u], 0.0).astype(BF16)).astype(BF16) for u in units}
        X = {u: X[u] - _dot(XC[u], Xb[u]) for u in units}
        s *= 2

    uu, wq, kd, sdec = {}, {}, {}, {}
    for u in units:
        exp_g = jnp.exp(G[u])
        sol = _dot(X[u].astype(BF16), jnp.concatenate([vb[u], kb[u] * exp_g], axis=1).astype(BF16))
        uu[u] = sol[:, :Dh]
        wq[u] = jnp.concatenate([sol[:, Dh:], qn[u] * exp_g], axis=0).astype(BF16)
        G_last = jnp.broadcast_to(G[u][Cn - 1:Cn, :], (Cn, Dh))
        kd[u] = (kn[u] * jnp.exp(G_last - G[u])).astype(BF16)
        e_last = jnp.exp(G_last)
        sdec[u] = jnp.concatenate([e_last, e_last], axis=0)

    states = {(b, h): state_ref[b * H + h] for b in range(nb) for h in range(H)}
    for c in range(nc):
        cur = [(b, c, h) for b in range(nb) for h in range(H)]
        r = {u: _dot(wq[u], states[u[0], u[2]].astype(BF16)) for u in cur}
        vnb = {u: (uu[u] - r[u][:Cn]).astype(BF16) for u in cur}
        for u in cur:
            b, _, h = u
            o = r[u][Cn:] + _dot(attn[u], vnb[u])
            states[b, h] = states[b, h] * sdec[u] + _dot_tn(kd[u], vnb[u])
            on = o * lax.rsqrt(jnp.mean(o * o, -1, keepdims=True) + RMS_EPS) * ong_ref[...]
            zh = z_ref[b, c * Cn:(c + 1) * Cn, h * Dh:(h + 1) * Dh]
            o_ref[b, c * Cn:(c + 1) * Cn, h * Dh:(h + 1) * Dh] = (on * _silu(zh)).astype(o_ref.dtype)

    for b in range(nb):
        for h in range(H):
            state_ref[b * H + h] = states[b, h]
        ext_ref[b, 0:_DN_HALO, :] = ext_ref[b, L:L + _DN_HALO, :]


def _deltanet(h, B, S, col0, sc_w, a_log, dt_bias, on_g, L):
    W = DN_HEADS * HEAD_DIM
    L = min(L, S)
    cb = col0 // W
    gb = (col0 + 4 * W) // LANES
    h3 = h.reshape(B, S, h.shape[1])
    scw = jnp.pad(sc_w, ((0, SUBLANES - SHORT_CONV), (0, 0)))
    lane_pad = lambda v: jnp.pad(v.reshape(1, -1), ((0, 0), (DN_HEADS, LANES - 2 * DN_HEADS)))
    vec = pl.BlockSpec((1, LANES), lambda s: (0, 0))
    col = lambda j: pl.BlockSpec((B, L, W), lambda s: (0, s, cb + j))
    out = pl.pallas_call(
        functools.partial(_deltanet_kernel, L=L),
        grid=(S // L,),
        in_specs=[col(0), col(1), col(2), col(3),
                  pl.BlockSpec((B, L, LANES), lambda s: (0, s, gb)),
                  pl.BlockSpec((SUBLANES, 3 * W), lambda s: (0, 0)),
                  vec, vec, vec],
        out_specs=pl.BlockSpec((B, L, W), lambda s: (0, s, 0)),
        out_shape=jax.ShapeDtypeStruct((B, S, W), BF16),
        scratch_shapes=[pltpu.VMEM((B, L + _DN_HALO, 3 * W), F32),
                        pltpu.VMEM((B * DN_HEADS, HEAD_DIM, HEAD_DIM), F32)],
        compiler_params=_cparams(1),
    )(h3, h3, h3, h3, h3, scw, lane_pad(a_log), lane_pad(dt_bias), on_g.reshape(1, HEAD_DIM))
    return out.reshape(B * S, W)


def _outproj_kernel(x_ref, yc_ref, yd_ref, w_ref, g_ref, b_ref, *rest, n_experts):
    Cc = yc_ref.shape[1]
    m = _dot(yc_ref[...], w_ref[0:Cc, :]) + _dot(yd_ref[...], w_ref[Cc:, :])
    x1 = _layer_norm(DEEPNORM_ALPHA * x_ref[...] + m, g_ref[...], b_ref[...])
    if not n_experts:
        (o_ref,) = rest
        o_ref[...] = x1
        return
    rw_ref, o_ref, meta_ref, gates_ref, cnt_ref, run_ref = rest
    o_ref[...] = x1
    tm = x1.shape[0]

    @pl.when(pl.program_id(0) == 0)
    def _():
        run_ref[...] = jnp.zeros(run_ref.shape, F32)

    logits = jnp.dot(x1, rw_ref[...], preferred_element_type=F32, precision=lax.Precision.HIGHEST)
    lane = lax.broadcasted_iota(I32, (tm, LANES), 1)
    lg = jnp.where(lane < n_experts, logits, -jnp.inf)
    v1 = jnp.max(lg, -1, keepdims=True)
    i1 = jnp.min(jnp.where(lg == v1, lane, LANES), -1, keepdims=True)
    lg2 = jnp.where(lane == i1, -jnp.inf, lg)
    v2 = jnp.max(lg2, -1, keepdims=True)
    i2 = jnp.min(jnp.where(lg2 == v2, lane, LANES), -1, keepdims=True)
    e2 = jnp.exp(v2 - v1)
    g1 = 1.0 / (1.0 + e2)
    g2 = e2 / (1.0 + e2)

    onehot = jnp.where(jnp.logical_or(lane == i1, lane == i2), 1.0, 0.0)
    ti = lax.broadcasted_iota(I32, (tm, tm), 0)
    tj = lax.broadcasted_iota(I32, (tm, tm), 1)
    before = jnp.where(tj < ti, 1.0, 0.0).astype(BF16)
    seen = _dot(before, onehot.astype(BF16)) + run_ref[0:1, :]
    r1 = jnp.sum(jnp.where(lane == i1, seen, 0.0), -1, keepdims=True).astype(I32)
    r2 = jnp.sum(jnp.where(lane == i2, seen, 0.0), -1, keepdims=True).astype(I32)
    run_ref[...] = run_ref[...] + jnp.sum(onehot, 0, keepdims=True)
    cnt_ref[...] = run_ref[...]

    meta_ref[...] = jnp.where(lane == 0, i1, jnp.where(lane == 1, i2, jnp.where(lane == 2, r1, jnp.where(lane == 3, r2, 0))))
    gates_ref[...] = jnp.where(lane == 0, g1, jnp.where(lane == 1, g2, 0.0))


def _out_proj(xt, yc, yd, w, g, b, router_w=None):
    T, D = xt.shape
    Cc, Cd = yc.shape[1], yd.shape[1]
    tm = min(512, T)
    vec = pl.BlockSpec((1, D), lambda i: (0, 0))
    rows = lambda n: pl.BlockSpec((tm, n), lambda i: (i, 0))
    in_specs = [rows(D), rows(Cc), rows(Cd), pl.BlockSpec((Cc + Cd, D), lambda i: (0, 0)), vec, vec]
    args = [xt, yc, yd, w, g.reshape(1, D), b.reshape(1, D)]
    if router_w is None:
        return pl.pallas_call(
            functools.partial(_outproj_kernel, n_experts=0),
            grid=(T // tm,), in_specs=in_specs, out_specs=rows(D),
            out_shape=jax.ShapeDtypeStruct((T, D), F32), compiler_params=_cparams(1),
        )(*args)
    E = router_w.shape[1]
    rw = jnp.pad(router_w, ((0, 0), (0, LANES - E)))
    return pl.pallas_call(
        functools.partial(_outproj_kernel, n_experts=E),
        grid=(T // tm,),
        in_specs=in_specs + [pl.BlockSpec((D, LANES), lambda i: (0, 0))],
        out_specs=[rows(D), rows(LANES), rows(LANES), pl.BlockSpec((SUBLANES, LANES), lambda i: (0, 0))],
        out_shape=[jax.ShapeDtypeStruct((T, D), F32), jax.ShapeDtypeStruct((T, LANES), I32),
                   jax.ShapeDtypeStruct((T, LANES), F32), jax.ShapeDtypeStruct((SUBLANES, LANES), F32)],
        scratch_shapes=[pltpu.VMEM((SUBLANES, LANES), F32)],
        compiler_params=_cparams(1),
    )(*args, rw)


_FF_CHUNK = 256


def _ffn_dense_kernel(x_ref, wg_ref, wu_ref, wd_ref, g_ref, b_ref, o_ref, xb_ref, *, tf):
    F = wg_ref.shape[1]
    xb_ref[...] = x_ref[...].astype(BF16)
    for f0 in range(0, F, tf):
        xb = xb_ref[...]
        gate = _dot(xb, wg_ref[:, f0:f0 + tf])
        up = _dot(xb, wu_ref[:, f0:f0 + tf])
        part = _dot((_silu(gate) * up).astype(BF16), wd_ref[f0:f0 + tf, :])
        if f0 == 0:
            o_ref[...] = part
        else:
            o_ref[...] = o_ref[...] + part
    o_ref[...] = _layer_norm(DEEPNORM_ALPHA * x_ref[...] + o_ref[...], g_ref[...], b_ref[...])


def _ff_tile(F, target):
    best = LANES
    for t in range(LANES, min(F, target) + 1, LANES):
        if F % t == 0:
            best = t
    return best


def _ffn_dense(xt, wg, wu, wd, g, b):
    T, D = xt.shape
    F = wg.shape[1]
    tm = min(1024, T)
    tf = _ff_tile(F, _FF_CHUNK)
    vec = pl.BlockSpec((1, D), lambda i: (0, 0))
    return pl.pallas_call(
        functools.partial(_ffn_dense_kernel, tf=tf),
        grid=(T // tm,),
        in_specs=[pl.BlockSpec((tm, D), lambda i: (i, 0)),
                  _resident((D, F), lambda i: (0, 0)),
                  _resident((D, F), lambda i: (0, 0)),
                  _resident((F, D), lambda i: (0, 0)),
                  vec, vec],
        out_specs=pl.BlockSpec((tm, D), lambda i: (i, 0)),
        out_shape=jax.ShapeDtypeStruct((T, D), F32),
        scratch_shapes=[pltpu.VMEM((tm, D), BF16)],
        compiler_params=_cparams(1),
    )(xt, wg, wu, wd, g.reshape(1, D), b.reshape(1, D))


def _moe_kernel(te_ref, nv_ref, src_ref, x_hbm, wg_ref, wu_ref, wd_ref, o_ref, xs_ref, xb_ref, acc_ref, sem):
    i = pl.program_id(0)
    f = pl.program_id(1)
    tm = xs_ref.shape[0]

    @pl.when(i < nv_ref[0])
    def _():
        @pl.when(f == 0)
        def _():
            def issue(r, carry):
                pltpu.make_async_copy(x_hbm.at[pl.ds(src_ref[0, 0, r], 1), :], xs_ref.at[pl.ds(r, 1), :], sem).start()
                return carry

            lax.fori_loop(0, tm, issue, 0)
            pltpu.make_async_copy(x_hbm.at[pl.ds(0, tm), :], xs_ref, sem).wait()
            xb_ref[...] = xs_ref[...].astype(BF16)

        xb = xb_ref[...]
        gate = _dot(xb, wg_ref[...])
        up = _dot(xb, wu_ref[...])
        part = _dot((_silu(gate) * up).astype(BF16), wd_ref[...])

        @pl.when(f == 0)
        def _():
            acc_ref[...] = part

        @pl.when(f > 0)
        def _():
            acc_ref[...] = acc_ref[...] + part

        @pl.when(f == pl.num_programs(1) - 1)
        def _():
            o_ref[...] = acc_ref[...]

    @pl.when(jnp.logical_and(i >= nv_ref[0], f == 0))
    def _():
        o_ref[...] = jnp.zeros(o_ref.shape, o_ref.dtype)


def _moe_experts(x1, src, tile_expert, n_valid, wg, wu, wd, tm):
    T, D = x1.shape
    F = wg.shape[2]
    n_tiles = src.shape[0]
    tf = _ff_tile(F, 896)
    grid_spec = pltpu.PrefetchScalarGridSpec(
        num_scalar_prefetch=2,
        grid=(n_tiles, F // tf),
        in_specs=[pl.BlockSpec((1, 1, tm), lambda i, f, te, nv: (i, 0, 0), memory_space=pltpu.SMEM),
                  pl.BlockSpec(memory_space=pl.ANY),
                  pl.BlockSpec((None, D, tf), lambda i, f, te, nv: (te[i], 0, f)),
                  pl.BlockSpec((None, D, tf), lambda i, f, te, nv: (te[i], 0, f)),
                  pl.BlockSpec((None, tf, D), lambda i, f, te, nv: (te[i], f, 0))],
        out_specs=pl.BlockSpec((tm, D), lambda i, f, te, nv: (i, 0)),
        scratch_shapes=[pltpu.VMEM((tm, D), F32), pltpu.VMEM((tm, D), BF16), pltpu.VMEM((tm, D), F32),
                        pltpu.SemaphoreType.DMA(())],
    )
    return pl.pallas_call(
        _moe_kernel, grid_spec=grid_spec,
        out_shape=jax.ShapeDtypeStruct((n_tiles * tm, D), F32),
        compiler_params=_cparams(2),
    )(tile_expert, n_valid, src, x1, wg, wu, wd)


def _combine_kernel(pos_ref, x_ref, gates_ref, ys_hbm, g_ref, b_ref, o_ref, a_ref, b2_ref, sem):
    tc = x_ref.shape[0]

    def issue(r, carry):
        pltpu.make_async_copy(ys_hbm.at[pl.ds(pos_ref[0, 0, 2 * r], 1), :], a_ref.at[pl.ds(r, 1), :], sem).start()
        pltpu.make_async_copy(ys_hbm.at[pl.ds(pos_ref[0, 0, 2 * r + 1], 1), :], b2_ref.at[pl.ds(r, 1), :], sem).start()
        return carry

    lax.fori_loop(0, tc, issue, 0)
    pltpu.make_async_copy(ys_hbm.at[pl.ds(0, tc), :], a_ref, sem).wait()
    pltpu.make_async_copy(ys_hbm.at[pl.ds(0, tc), :], b2_ref, sem).wait()
    gts = gates_ref[...]
    f = gts[:, 0:1] * a_ref[...] + gts[:, 1:2] * b2_ref[...]
    o_ref[...] = _layer_norm(DEEPNORM_ALPHA * x_ref[...] + f, g_ref[...], b_ref[...])


def _moe_combine(x1, pos, gates, ys, g, b):
    T, D = x1.shape
    tc = min(256, T)
    vec = pl.BlockSpec((1, D), lambda i: (0, 0))
    return pl.pallas_call(
        _combine_kernel,
        grid=(T // tc,),
        in_specs=[pl.BlockSpec((1, 1, TOP_K * tc), lambda i: (i, 0, 0), memory_space=pltpu.SMEM),
                  pl.BlockSpec((tc, D), lambda i: (i, 0)),
                  pl.BlockSpec((tc, LANES), lambda i: (i, 0)),
                  pl.BlockSpec(memory_space=pl.ANY),
                  vec, vec],
        out_specs=pl.BlockSpec((tc, D), lambda i: (i, 0)),
        out_shape=jax.ShapeDtypeStruct((T, D), F32),
        scratch_shapes=[pltpu.VMEM((tc, D), F32), pltpu.VMEM((tc, D), F32), pltpu.SemaphoreType.DMA(())],
        compiler_params=_cparams(1),
    )(pos.reshape(T // tc, 1, TOP_K * tc), x1, gates, ys, g.reshape(1, D), b.reshape(1, D))


def _moe_ffn(x1, meta, gates, counts, wg, wu, wd, g, b):
    T, D = x1.shape
    E = wg.shape[0]
    tm = min(1024, T)
    n_tiles = (TOP_K * T) // tm + E - 1
    cnt = counts[0, :E].astype(I32)
    tiles_e = (cnt + tm - 1) // tm
    tile_end = jnp.cumsum(tiles_e)
    row_off = (tile_end - tiles_e) * tm
    idx, rank = meta[:, 0:TOP_K], meta[:, TOP_K:2 * TOP_K]
    pos = row_off[idx] + rank
    tok = jnp.broadcast_to(jnp.arange(T, dtype=I32)[:, None], (T, TOP_K))
    src = jnp.zeros((n_tiles * tm,), I32).at[pos.reshape(-1)].set(tok.reshape(-1), unique_indices=True)
    tile_ids = jnp.arange(n_tiles, dtype=I32)
    tile_expert = jnp.minimum(jnp.sum((tile_ids[:, None] >= tile_end[None, :]).astype(I32), axis=1), E - 1)
    n_valid = tile_end[E - 1:E].astype(I32)
    ys = _moe_experts(x1, src.reshape(n_tiles, 1, tm), tile_expert, n_valid, wg, wu, wd, tm)
    return _moe_combine(x1, pos, gates, ys, g, b)


def kernel(x, w_in, conv_dw_w, conv_dw_b, conv_ln_g, conv_ln_b, short_conv_w, a_log, dt_bias, out_norm_g, w_out, ln_mix_g, ln_mix_b, ffn_w_gate, ffn_w_up, ffn_w_down, router_w, moe_w_gate, moe_w_up, moe_w_down, ln_ffn_g, ln_ffn_b):
    B, S, D = x.shape
    T = B * S
    xt = x.reshape(T, D)
    Cc = conv_dw_w.shape[2]
    in_cols = w_in.shape[2]
    in_cols_pad = -(-in_cols // LANES) * LANES
    for layer in range(w_in.shape[0]):
        w_in_p = jnp.pad(w_in[layer], ((0, 0), (0, in_cols_pad - in_cols))).astype(BF16)
        h = _in_proj(xt, w_in_p)
        yc = _conv_mixer(h, B, S, conv_dw_w[layer], conv_dw_b[layer], conv_ln_g[layer], conv_ln_b[layer])
        yd = _deltanet(h, B, S, 2 * Cc, short_conv_w[layer], a_log[layer], dt_bias[layer], out_norm_g[layer], L=256)
        j = layer // 2
        w_o = w_out[layer].astype(BF16)
        if layer % 2 == 0:
            x1 = _out_proj(xt, yc, yd, w_o, ln_mix_g[layer], ln_mix_b[layer])
            xt = _ffn_dense(x1, ffn_w_gate[j].astype(BF16), ffn_w_up[j].astype(BF16), ffn_w_down[j].astype(BF16),
                            ln_ffn_g[layer], ln_ffn_b[layer])
        else:
            x1, meta, gates, counts = _out_proj(xt, yc, yd, w_o, ln_mix_g[layer], ln_mix_b[layer], router_w[j])
            xt = _moe_ffn(x1, meta, gates, counts, moe_w_gate[j].astype(BF16), moe_w_up[j].astype(BF16),
                          moe_w_down[j].astype(BF16), ln_ffn_g[layer], ln_ffn_b[layer])
    return xt.reshape(B, S, D)
```

```python
import functools

import jax
import jax.numpy as jnp
from jax import lax
from jax.experimental import pallas as pl
from jax.experimental.pallas import tpu as pltpu

F32 = jnp.float32
BF16 = jnp.bfloat16
I32 = jnp.int32

LANES = 128
SUBLANES = 8
VMEM_LIMIT = 56 * 1024 * 1024

DEPTH = 2
CONV_WIN = 31
SHORT_CONV = 4
DN_HEADS = 4
HEAD_DIM = 128
CHUNK = 64
TOP_K = 2
DEEPNORM_ALPHA = (2 * DEPTH) ** 0.25
LN_EPS = 1e-5
RMS_EPS = 1e-6
L2_EPS = 1e-6


def _cparams(n_axes):
    return pltpu.CompilerParams(dimension_semantics=("arbitrary",) * n_axes,
                                vmem_limit_bytes=VMEM_LIMIT)


def _resident(shape, index_map):
    return pl.BlockSpec(shape, index_map, pipeline_mode=pl.Buffered(1))


def _sigmoid(x):
    return 1.0 / (1.0 + jnp.exp(-x))


def _silu(x):
    return x * _sigmoid(x)


def _layer_norm(y, g, b):
    mu = jnp.mean(y, -1, keepdims=True)
    d = y - mu
    var = jnp.mean(d * d, -1, keepdims=True)
    return d * lax.rsqrt(var + LN_EPS) * g + b


def _dot(a, b):
    return jnp.dot(a, b, preferred_element_type=F32)


def _dot_nt(a, b):
    return lax.dot_general(a, b, (((1,), (1,)), ((), ())), preferred_element_type=F32)


def _dot_tn(a, b):
    return lax.dot_general(a, b, (((0,), (0,)), ((), ())), preferred_element_type=F32)


def _inproj_kernel(x_ref, w_ref, o_ref):
    o_ref[...] = _dot(x_ref[...].astype(BF16), w_ref[...])


def _in_proj(xt, w):
    T, D = xt.shape
    N = w.shape[1]
    tm = min(512, T)
    return pl.pallas_call(
        _inproj_kernel,
        grid=(T // tm,),
        in_specs=[pl.BlockSpec((tm, D), lambda i: (i, 0)),
                  _resident((D, N), lambda i: (0, 0))],
        out_specs=pl.BlockSpec((tm, N), lambda i: (i, 0)),
        out_shape=jax.ShapeDtypeStruct((T, N), F32),
        compiler_params=_cparams(1),
    )(xt, w)


_CONV_HALO = 32
_CONV_SUB = 64


def _convmix_kernel(val_ref, gate_ref, w_ref, b_ref, g_ref, bb_ref, o_ref, ext_ref, sh_ref, *, ts):
    C = val_ref.shape[1]

    @pl.when(pl.program_id(1) == 0)
    def _():
        ext_ref[0:_CONV_HALO, :] = jnp.zeros((_CONV_HALO, C), F32)

    ext_ref[_CONV_HALO:_CONV_HALO + ts, :] = val_ref[...] * _sigmoid(gate_ref[...])
    for r in range(1, SUBLANES):
        sh_ref[r - 1, :, :] = ext_ref[r:r + ts + 24, :]

    first = _CONV_HALO - (CONV_WIN - 1)

    def body(i, carry):
        base = pl.multiple_of(i * _CONV_SUB, _CONV_SUB)
        acc = jnp.zeros((_CONV_SUB, C), F32) + b_ref[...]
        for j in range(CONV_WIN):
            a, r = divmod(first + j, SUBLANES)
            if r == 0:
                win = ext_ref[pl.ds(base + SUBLANES * a, _CONV_SUB), :]
            else:
                win = sh_ref[r - 1, pl.ds(base + SUBLANES * a, _CONV_SUB), :]
            acc = acc + w_ref[j:j + 1, :] * win
        y = _layer_norm(acc, g_ref[...], bb_ref[...])
        o_ref[pl.ds(base, _CONV_SUB), :] = _silu(y).astype(o_ref.dtype)
        return carry

    lax.fori_loop(0, ts // _CONV_SUB, body, 0)
    ext_ref[0:_CONV_HALO, :] = ext_ref[ts:ts + _CONV_HALO, :]


def _conv_mixer(h, B, S, dw_w, dw_b, ln_g, ln_b):
    C = dw_w.shape[1]
    T = B * S
    ts = min(512, S)
    ns = S // ts
    w = jnp.pad(dw_w, ((0, 32 - CONV_WIN), (0, 0)))
    row = lambda b, s: b * ns + s
    vec = pl.BlockSpec((1, C), lambda b, s: (0, 0))
    return pl.pallas_call(
        functools.partial(_convmix_kernel, ts=ts),
        grid=(B, ns),
        in_specs=[pl.BlockSpec((ts, C), lambda b, s: (row(b, s), 0)),
                  pl.BlockSpec((ts, C), lambda b, s: (row(b, s), 1)),
                  pl.BlockSpec((32, C), lambda b, s: (0, 0)),
                  vec, vec, vec],
        out_specs=pl.BlockSpec((ts, C), lambda b, s: (row(b, s), 0)),
        out_shape=jax.ShapeDtypeStruct((T, C), BF16),
        scratch_shapes=[pltpu.VMEM((ts + _CONV_HALO, C), F32),
                        pltpu.VMEM((SUBLANES - 1, ts + 24, C), F32)],
        compiler_params=_cparams(2),
    )(h, h, w, dw_b.reshape(1, C), ln_g.reshape(1, C), ln_b.reshape(1, C))


_DN_HALO = 8


def _split3(a):
    hi = a.astype(BF16)
    r1 = a - hi.astype(F32)
    mid = r1.astype(BF16)
    lo = (r1 - mid.astype(F32)).astype(BF16)
    return hi, mid, lo


def _deltanet_kernel(q_ref, k_ref, v_ref, z_ref, gt_ref, scw_ref, alog_ref, dtb_ref, ong_ref, o_ref,
                     ext_ref, state_ref, *, L):
    nb, _, W = q_ref.shape
    Cn, Dh, H = CHUNK, HEAD_DIM, DN_HEADS
    nc = L // Cn

    @pl.when(pl.program_id(0) == 0)
    def _():
        ext_ref[:, 0:_DN_HALO, :] = jnp.zeros((nb, _DN_HALO, 3 * W), F32)
        state_ref[...] = jnp.zeros(state_ref.shape, F32)

    for b in range(nb):
        ext_ref[b, _DN_HALO:_DN_HALO + L, 0:W] = q_ref[b]
        ext_ref[b, _DN_HALO:_DN_HALO + L, W:2 * W] = k_ref[b]
        ext_ref[b, _DN_HALO:_DN_HALO + L, 2 * W:3 * W] = v_ref[b]

    i6 = lax.broadcasted_iota(I32, (Cn, Cn), 0)
    j6 = lax.broadcasted_iota(I32, (Cn, Cn), 1)
    causal = i6 >= j6
    strict = i6 > j6
    eye = jnp.where(i6 == j6, 1.0, 0.0).astype(F32)
    i7 = lax.broadcasted_iota(I32, (Cn, 2 * Cn), 0)
    j7 = lax.broadcasted_iota(I32, (Cn, 2 * Cn), 1)
    after = jnp.logical_or(i7 > j7, j7 >= Cn)
    i9 = lax.broadcasted_iota(I32, (Cn, 3 * Cn), 0)
    j9 = lax.broadcasted_iota(I32, (Cn, 3 * Cn), 1)
    tri3 = jnp.where((j9 & (Cn - 1)) <= i9, 1.0, 0.0).astype(BF16)

    def level_mask(s):
        same = ((i6 ^ j6) >> (s.bit_length())) == 0
        return jnp.logical_and(same, jnp.logical_and((i6 & s) != 0, (j6 & s) == 0))

    first = _DN_HALO - (SHORT_CONV - 1)

    def conv_act(b, r0, c0):
        acc = None
        for j in range(SHORT_CONV):
            t = scw_ref[j:j + 1, c0:c0 + Dh] * ext_ref[b, first + j + r0:first + j + r0 + Cn, c0:c0 + Dh]
            acc = t if acc is None else acc + t
        return _silu(acc)

    tiles = [(b, c) for b in range(nb) for c in range(nc)]
    units = [(b, c, h) for (b, c) in tiles for h in range(H)]

    beta_all, g_all = {}, {}
    for (b, c) in tiles:
        gt = gt_ref[b, c * Cn:(c + 1) * Cn, :]
        beta_all[b, c] = _sigmoid(gt)
        sp_in = gt + dtb_ref[...]
        softplus = jnp.maximum(sp_in, 0.0) + jnp.log1p(jnp.exp(-jnp.abs(sp_in)))
        g_all[b, c] = -jnp.exp(alog_ref[...]) * softplus

    dm = {}
    for (b, c, h) in units:
        g_b = jnp.broadcast_to(g_all[b, c][:, H + h:H + h + 1], (Cn, 2 * Cn))
        hi, mid, lo = _split3(jnp.where(after, g_b, 0.0))
        dm[b, c, h] = _dot(tri3, jnp.concatenate([hi, mid, lo], axis=0))

    qn, kn, kb, vb = {}, {}, {}, {}
    for (b, c, h) in units:
        r0, c0 = c * Cn, h * Dh
        qh = conv_act(b, r0, c0)
        kh = conv_act(b, r0, W + c0)
        vh = conv_act(b, r0, 2 * W + c0)
        qn[b, c, h] = qh * lax.rsqrt(jnp.sum(qh * qh, -1, keepdims=True) + L2_EPS) * (Dh ** -0.5)
        kh = kh * lax.rsqrt(jnp.sum(kh * kh, -1, keepdims=True) + L2_EPS)
        beta_b = jnp.broadcast_to(beta_all[b, c][:, h:h + 1], (Cn, Dh))
        kn[b, c, h] = kh
        kb[b, c, h] = kh * beta_b
        vb[b, c, h] = vh * beta_b

    decay, G = {}, {}
    for u in units:
        decay[u] = jnp.where(causal, jnp.exp(dm[u][:, :Cn]), 0.0)
        G[u] = jnp.broadcast_to(dm[u][:, Cn:Cn + 1], (Cn, Dh))

    lower, attn = {}, {}
    for u in units:
        kh_bf = kn[u].astype(BF16)
        lower[u] = jnp.where(strict, _dot_nt(kb[u].astype(BF16), kh_bf) * decay[u], 0.0)
        attn[u] = (_dot_nt(qn[u].astype(BF16), kh_bf) * decay[u]).astype(BF16)

    m1 = level_mask(1)
    X = {u: eye - jnp.where(m1, lower[u], 0.0) for u in units}
    s = 2
    while s < Cn:
        ms = level_mask(s)
        Xb = {u: X[u].astype(BF16) for u in units}
        XC = {u: _dot(Xb[u], jnp.where(ms, lower[u], 0.0).astype(BF16)).astype(BF16) for u in units}
        X = {u: X[u] - _dot(XC[u], Xb[u]) for u in units}
        s *= 2

    uu, wq, kd, sdec = {}, {}, {}, {}
    for u in units:
        exp_g = jnp.exp(G[u])
        sol = _dot(X[u].astype(BF16), jnp.concatenate([vb[u], kb[u] * exp_g], axis=1).astype(BF16))
        uu[u] = sol[:, :Dh]
        wq[u] = jnp.concatenate([sol[:, Dh:], qn[u] * exp_g], axis=0).astype(BF16)
        G_last = jnp.broadcast_to(G[u][Cn - 1:Cn, :], (Cn, Dh))
        kd[u] = (kn[u] * jnp.exp(G_last - G[u])).astype(BF16)
        e_last = jnp.exp(G_last)
        sdec[u] = jnp.concatenate([e_last, e_last], axis=0)

    states = {(b, h): state_ref[b * H + h] for b in range(nb) for h in range(H)}
    for c in range(nc):
        cur = [(b, c, h) for b in range(nb) for h in range(H)]
        r = {u: _dot(wq[u], states[u[0], u[2]].astype(BF16)) for u in cur}
        vnb = {u: (uu[u] - r[u][:Cn]).astype(BF16) for u in cur}
        for u in cur:
            b, _, h = u
            o = r[u][Cn:] + _dot(attn[u], vnb[u])
            states[b, h] = states[b, h] * sdec[u] + _dot_tn(kd[u], vnb[u])
            on = o * lax.rsqrt(jnp.mean(o * o, -1, keepdims=True) + RMS_EPS) * ong_ref[...]
            zh = z_ref[b, c * Cn:(c + 1) * Cn, h * Dh:(h + 1) * Dh]
            o_ref[b, c * Cn:(c + 1) * Cn, h * Dh:(h + 1) * Dh] = (on * _silu(zh)).astype(o_ref.dtype)

    for b in range(nb):
        for h in range(H):
            state_ref[b * H + h] = states[b, h]
        ext_ref[b, 0:_DN_HALO, :] = ext_ref[b, L:L + _DN_HALO, :]


def _deltanet(h, B, S, col0, sc_w, a_log, dt_bias, on_g, L):
    W = DN_HEADS * HEAD_DIM
    L = min(L, S)
    cb = col0 // W
    gb = (col0 + 4 * W) // LANES
    h3 = h.reshape(B, S, h.shape[1])
    scw = jnp.pad(sc_w, ((0, SUBLANES - SHORT_CONV), (0, 0)))
    lane_pad = lambda v: jnp.pad(v.reshape(1, -1), ((0, 0), (DN_HEADS, LANES - 2 * DN_HEADS)))
    vec = pl.BlockSpec((1, LANES), lambda s: (0, 0))
    col = lambda j: pl.BlockSpec((B, L, W), lambda s: (0, s, cb + j))
    out = pl.pallas_call(
        functools.partial(_deltanet_kernel, L=L),
        grid=(S // L,),
        in_specs=[col(0), col(1), col(2), col(3),
                  pl.BlockSpec((B, L, LANES), lambda s: (0, s, gb)),
                  pl.BlockSpec((SUBLANES, 3 * W), lambda s: (0, 0)),
                  vec, vec, vec],
        out_specs=pl.BlockSpec((B, L, W), lambda s: (0, s, 0)),
        out_shape=jax.ShapeDtypeStruct((B, S, W), BF16),
        scratch_shapes=[pltpu.VMEM((B, L + _DN_HALO, 3 * W), F32),
                        pltpu.VMEM((B * DN_HEADS, HEAD_DIM, HEAD_DIM), F32)],
        compiler_params=_cparams(1),
    )(h3, h3, h3, h3, h3, scw, lane_pad(a_log), lane_pad(dt_bias), on_g.reshape(1, HEAD_DIM))
    return out.reshape(B * S, W)


def _outproj_kernel(x_ref, yc_ref, yd_ref, w_ref, g_ref, b_ref, *rest, n_experts):
    Cc = yc_ref.shape[1]
    m = _dot(yc_ref[...], w_ref[0:Cc, :]) + _dot(yd_ref[...], w_ref[Cc:, :])
    x1 = _layer_norm(DEEPNORM_ALPHA * x_ref[...] + m, g_ref[...], b_ref[...])
    if not n_experts:
        (o_ref,) = rest
        o_ref[...] = x1
        return
    rw_ref, o_ref, meta_ref, gates_ref, cnt_ref, run_ref = rest
    o_ref[...] = x1
    tm = x1.shape[0]

    @pl.when(pl.program_id(0) == 0)
    def _():
        run_ref[...] = jnp.zeros(run_ref.shape, F32)

    xh, xm, _ = _split3(x1)
    wh, wm, _ = _split3(rw_ref[...])
    logits = _dot(xh, wh) + (_dot(xh, wm) + _dot(xm, wh))
    lane = lax.broadcasted_iota(I32, (tm, LANES), 1)
    lg = jnp.where(lane < n_experts, logits, -jnp.inf)
    v1 = jnp.max(lg, -1, keepdims=True)
    i1 = jnp.min(jnp.where(lg == v1, lane, LANES), -1, keepdims=True)
    lg2 = jnp.where(lane == i1, -jnp.inf, lg)
    v2 = jnp.max(lg2, -1, keepdims=True)
    i2 = jnp.min(jnp.where(lg2 == v2, lane, LANES), -1, keepdims=True)
    e2 = jnp.exp(v2 - v1)
    g1 = 1.0 / (1.0 + e2)
    g2 = e2 / (1.0 + e2)

    onehot = jnp.where(jnp.logical_or(lane == i1, lane == i2), 1.0, 0.0)
    ti = lax.broadcasted_iota(I32, (tm, tm), 0)
    tj = lax.broadcasted_iota(I32, (tm, tm), 1)
    before = jnp.where(tj < ti, 1.0, 0.0).astype(BF16)
    seen = _dot(before, onehot.astype(BF16)) + run_ref[0:1, :]
    r1 = jnp.sum(jnp.where(lane == i1, seen, 0.0), -1, keepdims=True).astype(I32)
    r2 = jnp.sum(jnp.where(lane == i2, seen, 0.0), -1, keepdims=True).astype(I32)
    run_ref[...] = run_ref[...] + jnp.sum(onehot, 0, keepdims=True)
    cnt_ref[...] = run_ref[...]

    meta_ref[...] = jnp.where(lane == 0, i1, jnp.where(lane == 1, i2, jnp.where(lane == 2, r1, jnp.where(lane == 3, r2, 0))))
    gates_ref[...] = jnp.where(lane == 0, g1, jnp.where(lane == 1, g2, 0.0))


def _out_proj(xt, yc, yd, w, g, b, router_w=None):
    T, D = xt.shape
    Cc, Cd = yc.shape[1], yd.shape[1]
    tm = min(512, T)
    vec = pl.BlockSpec((1, D), lambda i: (0, 0))
    rows = lambda n: pl.BlockSpec((tm, n), lambda i: (i, 0))
    in_specs = [rows(D), rows(Cc), rows(Cd), pl.BlockSpec((Cc + Cd, D), lambda i: (0, 0)), vec, vec]
    args = [xt, yc, yd, w, g.reshape(1, D), b.reshape(1, D)]
    if router_w is None:
        return pl.pallas_call(
            functools.partial(_outproj_kernel, n_experts=0),
            grid=(T // tm,), in_specs=in_specs, out_specs=rows(D),
            out_shape=jax.ShapeDtypeStruct((T, D), F32), compiler_params=_cparams(1),
        )(*args)
    E = router_w.shape[1]
    rw = jnp.pad(router_w, ((0, 0), (0, LANES - E)))
    return pl.pallas_call(
        functools.partial(_outproj_kernel, n_experts=E),
        grid=(T // tm,),
        in_specs=in_specs + [pl.BlockSpec((D, LANES), lambda i: (0, 0))],
        out_specs=[rows(D), rows(LANES), rows(LANES), pl.BlockSpec((SUBLANES, LANES), lambda i: (0, 0))],
        out_shape=[jax.ShapeDtypeStruct((T, D), F32), jax.ShapeDtypeStruct((T, LANES), I32),
                   jax.ShapeDtypeStruct((T, LANES), F32), jax.ShapeDtypeStruct((SUBLANES, LANES), F32)],
        scratch_shapes=[pltpu.VMEM((SUBLANES, LANES), F32)],
        compiler_params=_cparams(1),
    )(*args, rw)


_FF_CHUNK = 256


def _ffn_dense_kernel(x_ref, wg_ref, wu_ref, wd_ref, g_ref, b_ref, o_ref, xb_ref, *, tf):
    F = wg_ref.shape[1]
    xb_ref[...] = x_ref[...].astype(BF16)
    for f0 in range(0, F, tf):
        xb = xb_ref[...]
        gate = _dot(xb, wg_ref[:, f0:f0 + tf])
        up = _dot(xb, wu_ref[:, f0:f0 + tf])
        part = _dot((_silu(gate) * up).astype(BF16), wd_ref[f0:f0 + tf, :])
        if f0 == 0:
            o_ref[...] = part
        else:
            o_ref[...] = o_ref[...] + part
    o_ref[...] = _layer_norm(DEEPNORM_ALPHA * x_ref[...] + o_ref[...], g_ref[...], b_ref[...])


def _ff_tile(F, target):
    best = LANES
    for t in range(LANES, min(F, target) + 1, LANES):
        if F % t == 0:
            best = t
    return best


def _ffn_dense(xt, wg, wu, wd, g, b):
    T, D = xt.shape
    F = wg.shape[1]
    tm = min(1024, T)
    tf = _ff_tile(F, _FF_CHUNK)
    vec = pl.BlockSpec((1, D), lambda i: (0, 0))
    return pl.pallas_call(
        functools.partial(_ffn_dense_kernel, tf=tf),
        grid=(T // tm,),
        in_specs=[pl.BlockSpec((tm, D), lambda i: (i, 0)),
                  _resident((D, F), lambda i: (0, 0)),
                  _resident((D, F), lambda i: (0, 0)),
                  _resident((F, D), lambda i: (0, 0)),
                  vec, vec],
        out_specs=pl.BlockSpec((tm, D), lambda i: (i, 0)),
        out_shape=jax.ShapeDtypeStruct((T, D), F32),
        scratch_shapes=[pltpu.VMEM((tm, D), BF16)],
        compiler_params=_cparams(1),
    )(xt, wg, wu, wd, g.reshape(1, D), b.reshape(1, D))


_GATHER_UNROLL = 8


def _gather_rows(src_hbm, idx_ref, n_rows, stride, offset, dst_ref, sem):
    def issue(r, carry):
        row = idx_ref[0, 0, stride * r + offset]
        pltpu.make_async_copy(src_hbm.at[pl.ds(row, 1), :], dst_ref.at[pl.ds(r, 1), :], sem).start()
        return carry

    lax.fori_loop(0, n_rows, issue, 0, unroll=_GATHER_UNROLL)


def _wait_rows(src_hbm, dst_ref, sem):
    pltpu.make_async_copy(src_hbm.at[pl.ds(0, dst_ref.shape[0]), :], dst_ref, sem).wait()


def _moe_kernel(te_ref, nv_ref, src_ref, nxt_ref, x_hbm, wg_ref, wu_ref, wd_ref, o_ref, xs_ref, xb_ref, sems, *, tf):
    i = pl.program_id(0)
    tm = xs_ref.shape[1]
    F = wg_ref.shape[1]
    slot = i % 2

    @pl.when(jnp.logical_and(i == 0, nv_ref[0] > 0))
    def _():
        _gather_rows(x_hbm, src_ref, tm, 1, 0, xs_ref.at[0], sems.at[0])

    @pl.when(i + 1 < nv_ref[0])
    def _():
        _gather_rows(x_hbm, nxt_ref, tm, 1, 0, xs_ref.at[1 - slot], sems.at[1 - slot])

    @pl.when(i < nv_ref[0])
    def _():
        _wait_rows(x_hbm, xs_ref.at[slot], sems.at[slot])
        xb_ref[...] = xs_ref[slot].astype(BF16)
        for f0 in range(0, F, tf):
            xb = xb_ref[...]
            gate = _dot(xb, wg_ref[:, f0:f0 + tf])
            up = _dot(xb, wu_ref[:, f0:f0 + tf])
            part = _dot((_silu(gate) * up).astype(BF16), wd_ref[f0:f0 + tf, :])
            if f0 == 0:
                o_ref[...] = part
            else:
                o_ref[...] = o_ref[...] + part

    @pl.when(i >= nv_ref[0])
    def _():
        o_ref[...] = jnp.zeros(o_ref.shape, o_ref.dtype)


def _moe_experts(x1, src, tile_expert, n_valid, wg, wu, wd, tm):
    T, D = x1.shape
    F = wg.shape[2]
    n_tiles = src.shape[0]
    tf = _ff_tile(F, _FF_CHUNK)
    idx_spec = lambda step: pl.BlockSpec((1, 1, tm), lambda i, te, nv: (jnp.minimum(i + step, n_tiles - 1), 0, 0),
                                         memory_space=pltpu.SMEM)
    expert_w = lambda shape: pl.BlockSpec((None,) + shape, lambda i, te, nv: (te[i], 0, 0),
                                          pipeline_mode=pl.Buffered(1))
    grid_spec = pltpu.PrefetchScalarGridSpec(
        num_scalar_prefetch=2,
        grid=(n_tiles,),
        in_specs=[idx_spec(0), idx_spec(1),
                  pl.BlockSpec(memory_space=pl.ANY),
                  expert_w((D, F)), expert_w((D, F)), expert_w((F, D))],
        out_specs=pl.BlockSpec((tm, D), lambda i, te, nv: (i, 0)),
        scratch_shapes=[pltpu.VMEM((2, tm, D), F32), pltpu.VMEM((tm, D), BF16),
                        pltpu.SemaphoreType.DMA((2,))],
    )
    return pl.pallas_call(
        functools.partial(_moe_kernel, tf=tf), grid_spec=grid_spec,
        out_shape=jax.ShapeDtypeStruct((n_tiles * tm, D), F32),
        compiler_params=_cparams(1),
    )(tile_expert, n_valid, src, src, x1, wg, wu, wd)


def _combine_kernel(pos_ref, nxt_ref, x_ref, gates_ref, ys_hbm, g_ref, b_ref, o_ref, a_ref, b2_ref, sems):
    i = pl.program_id(0)
    tc = x_ref.shape[0]
    slot = i % 2

    def gather(idx_ref, s):
        _gather_rows(ys_hbm, idx_ref, tc, TOP_K, 0, a_ref.at[s], sems.at[0, s])
        _gather_rows(ys_hbm, idx_ref, tc, TOP_K, 1, b2_ref.at[s], sems.at[1, s])

    @pl.when(i == 0)
    def _():
        gather(pos_ref, 0)

    @pl.when(i + 1 < pl.num_programs(0))
    def _():
        gather(nxt_ref, 1 - slot)

    _wait_rows(ys_hbm, a_ref.at[slot], sems.at[0, slot])
    _wait_rows(ys_hbm, b2_ref.at[slot], sems.at[1, slot])
    gts = gates_ref[...]
    f = gts[:, 0:1] * a_ref[slot] + gts[:, 1:2] * b2_ref[slot]
    o_ref[...] = _layer_norm(DEEPNORM_ALPHA * x_ref[...] + f, g_ref[...], b_ref[...])


def _moe_combine(x1, pos, gates, ys, g, b):
    T, D = x1.shape
    tc = min(256, T)
    n = T // tc
    vec = pl.BlockSpec((1, D), lambda i: (0, 0))
    idx_spec = lambda step: pl.BlockSpec((1, 1, TOP_K * tc), lambda i: (jnp.minimum(i + step, n - 1), 0, 0),
                                         memory_space=pltpu.SMEM)
    pos3 = pos.reshape(n, 1, TOP_K * tc)
    return pl.pallas_call(
        _combine_kernel,
        grid=(n,),
        in_specs=[idx_spec(0), idx_spec(1),
                  pl.BlockSpec((tc, D), lambda i: (i, 0)),
                  pl.BlockSpec((tc, LANES), lambda i: (i, 0)),
                  pl.BlockSpec(memory_space=pl.ANY),
                  vec, vec],
        out_specs=pl.BlockSpec((tc, D), lambda i: (i, 0)),
        out_shape=jax.ShapeDtypeStruct((T, D), F32),
        scratch_shapes=[pltpu.VMEM((2, tc, D), F32), pltpu.VMEM((2, tc, D), F32),
                        pltpu.SemaphoreType.DMA((2, 2))],
        compiler_params=_cparams(1),
    )(pos3, pos3, x1, gates, ys, g.reshape(1, D), b.reshape(1, D))


def _moe_ffn(x1, meta, gates, counts, wg, wu, wd, g, b):
    T, D = x1.shape
    E = wg.shape[0]
    tm = min(1024, T)
    n_tiles = (TOP_K * T) // tm + E - 1
    cnt = counts[0, :E].astype(I32)
    tiles_e = (cnt + tm - 1) // tm
    tile_end = jnp.cumsum(tiles_e)
    row_off = (tile_end - tiles_e) * tm
    idx, rank = meta[:, 0:TOP_K], meta[:, TOP_K:2 * TOP_K]
    pos = row_off[idx] + rank
    tok = jnp.broadcast_to(jnp.arange(T, dtype=I32)[:, None], (T, TOP_K))
    src = jnp.zeros((n_tiles * tm,), I32).at[pos.reshape(-1)].set(tok.reshape(-1), unique_indices=True)
    tile_ids = jnp.arange(n_tiles, dtype=I32)
    tile_expert = jnp.minimum(jnp.sum((tile_ids[:, None] >= tile_end[None, :]).astype(I32), axis=1), E - 1)
    n_valid = tile_end[E - 1:E].astype(I32)
    ys = _moe_experts(x1, src.reshape(n_tiles, 1, tm), tile_expert, n_valid, wg, wu, wd, tm)
    return _moe_combine(x1, pos, gates, ys, g, b)


def kernel(x, w_in, conv_dw_w, conv_dw_b, conv_ln_g, conv_ln_b, short_conv_w, a_log, dt_bias, out_norm_g, w_out, ln_mix_g, ln_mix_b, ffn_w_gate, ffn_w_up, ffn_w_down, router_w, moe_w_gate, moe_w_up, moe_w_down, ln_ffn_g, ln_ffn_b):
    B, S, D = x.shape
    T = B * S
    xt = x.reshape(T, D)
    Cc = conv_dw_w.shape[2]
    in_cols = w_in.shape[2]
    in_cols_pad = -(-in_cols // LANES) * LANES
    for layer in range(w_in.shape[0]):
        w_in_p = jnp.pad(w_in[layer], ((0, 0), (0, in_cols_pad - in_cols))).astype(BF16)
        h = _in_proj(xt, w_in_p)
        yc = _conv_mixer(h, B, S, conv_dw_w[layer], conv_dw_b[layer], conv_ln_g[layer], conv_ln_b[layer])
        yd = _deltanet(h, B, S, 2 * Cc, short_conv_w[layer], a_log[layer], dt_bias[layer], out_norm_g[layer], L=256)
        j = layer // 2
        w_o = w_out[layer].astype(BF16)
        if layer % 2 == 0:
            x1 = _out_proj(xt, yc, yd, w_o, ln_mix_g[layer], ln_mix_b[layer])
            xt = _ffn_dense(x1, ffn_w_gate[j].astype(BF16), ffn_w_up[j].astype(BF16), ffn_w_down[j].astype(BF16),
                            ln_ffn_g[layer], ln_ffn_b[layer])
        else:
            x1, meta, gates, counts = _out_proj(xt, yc, yd, w_o, ln_mix_g[layer], ln_mix_b[layer], router_w[j])
            xt = _moe_ffn(x1, meta, gates, counts, moe_w_gate[j].astype(BF16), moe_w_up[j].astype(BF16),
                          moe_w_down[j].astype(BF16), ln_ffn_g[layer], ln_ffn_b[layer])
    return xt.reshape(B, S, D)
```

```python
import functools

import jax
import jax.numpy as jnp
from jax import lax
from jax.experimental import pallas as pl
from jax.experimental.pallas import tpu as pltpu

F32 = jnp.float32
BF16 = jnp.bfloat16
I32 = jnp.int32

LANES = 128
SUBLANES = 8
VMEM_LIMIT = 56 * 1024 * 1024

DEPTH = 2
CONV_WIN = 31
SHORT_CONV = 4
DN_HEADS = 4
HEAD_DIM = 128
CHUNK = 64
TOP_K = 2
DEEPNORM_ALPHA = (2 * DEPTH) ** 0.25
LN_EPS = 1e-5
RMS_EPS = 1e-6
L2_EPS = 1e-6


def _cparams(n_axes):
    return pltpu.CompilerParams(dimension_semantics=("arbitrary",) * n_axes,
                                vmem_limit_bytes=VMEM_LIMIT)


def _resident(shape, index_map):
    return pl.BlockSpec(shape, index_map, pipeline_mode=pl.Buffered(1))


def _sigmoid(x):
    return 1.0 / (1.0 + jnp.exp(-x))


def _silu(x):
    return x * _sigmoid(x)


def _layer_norm(y, g, b):
    mu = jnp.mean(y, -1, keepdims=True)
    d = y - mu
    var = jnp.mean(d * d, -1, keepdims=True)
    return d * lax.rsqrt(var + LN_EPS) * g + b


def _dot(a, b):
    return jnp.dot(a, b, preferred_element_type=F32)


def _dot_nt(a, b):
    return lax.dot_general(a, b, (((1,), (1,)), ((), ())), preferred_element_type=F32)


def _dot_tn(a, b):
    return lax.dot_general(a, b, (((0,), (0,)), ((), ())), preferred_element_type=F32)


def _inproj_kernel(x_ref, w_ref, o_ref):
    o_ref[...] = _dot(x_ref[...].astype(BF16), w_ref[...])


def _in_proj(xt, w):
    T, D = xt.shape
    N = w.shape[1]
    tm = min(1024, T)
    return pl.pallas_call(
        _inproj_kernel,
        grid=(T // tm,),
        in_specs=[pl.BlockSpec((tm, D), lambda i: (i, 0)),
                  _resident((D, N), lambda i: (0, 0))],
        out_specs=pl.BlockSpec((tm, N), lambda i: (i, 0)),
        out_shape=jax.ShapeDtypeStruct((T, N), F32),
        compiler_params=_cparams(1),
    )(xt, w)


_CONV_HALO = 32
_CONV_SUB = 64


def _convmix_kernel(val_ref, gate_ref, w_ref, b_ref, g_ref, bb_ref, o_ref, ext_ref, sh_ref, *, ts):
    C = val_ref.shape[1]

    @pl.when(pl.program_id(1) == 0)
    def _():
        ext_ref[0:_CONV_HALO, :] = jnp.zeros((_CONV_HALO, C), F32)

    ext_ref[_CONV_HALO:_CONV_HALO + ts, :] = val_ref[...] * _sigmoid(gate_ref[...])
    for r in range(1, SUBLANES):
        sh_ref[r - 1, :, :] = ext_ref[r:r + ts + 24, :]

    first = _CONV_HALO - (CONV_WIN - 1)

    def body(i, carry):
        base = pl.multiple_of(i * _CONV_SUB, _CONV_SUB)
        acc = jnp.zeros((_CONV_SUB, C), F32) + b_ref[...]
        for j in range(CONV_WIN):
            a, r = divmod(first + j, SUBLANES)
            if r == 0:
                win = ext_ref[pl.ds(base + SUBLANES * a, _CONV_SUB), :]
            else:
                win = sh_ref[r - 1, pl.ds(base + SUBLANES * a, _CONV_SUB), :]
            acc = acc + w_ref[j:j + 1, :] * win
        y = _layer_norm(acc, g_ref[...], bb_ref[...])
        o_ref[pl.ds(base, _CONV_SUB), :] = _silu(y).astype(o_ref.dtype)
        return carry

    lax.fori_loop(0, ts // _CONV_SUB, body, 0)
    ext_ref[0:_CONV_HALO, :] = ext_ref[ts:ts + _CONV_HALO, :]


def _conv_mixer(h, B, S, dw_w, dw_b, ln_g, ln_b):
    C = dw_w.shape[1]
    T = B * S
    ts = min(512, S)
    ns = S // ts
    w = jnp.pad(dw_w, ((0, 32 - CONV_WIN), (0, 0)))
    row = lambda b, s: b * ns + s
    vec = pl.BlockSpec((1, C), lambda b, s: (0, 0))
    return pl.pallas_call(
        functools.partial(_convmix_kernel, ts=ts),
        grid=(B, ns),
        in_specs=[pl.BlockSpec((ts, C), lambda b, s: (row(b, s), 0)),
                  pl.BlockSpec((ts, C), lambda b, s: (row(b, s), 1)),
                  pl.BlockSpec((32, C), lambda b, s: (0, 0)),
                  vec, vec, vec],
        out_specs=pl.BlockSpec((ts, C), lambda b, s: (row(b, s), 0)),
        out_shape=jax.ShapeDtypeStruct((T, C), BF16),
        scratch_shapes=[pltpu.VMEM((ts + _CONV_HALO, C), F32),
                        pltpu.VMEM((SUBLANES - 1, ts + 24, C), F32)],
        compiler_params=_cparams(2),
    )(h, h, w, dw_b.reshape(1, C), ln_g.reshape(1, C), ln_b.reshape(1, C))


_DN_HALO = 8


def _split3(a):
    hi = a.astype(BF16)
    r1 = a - hi.astype(F32)
    mid = r1.astype(BF16)
    lo = (r1 - mid.astype(F32)).astype(BF16)
    return hi, mid, lo


def _deltanet_kernel(q_ref, k_ref, v_ref, z_ref, gt_ref, scw_ref, alog_ref, dtb_ref, ong_ref, o_ref,
                     ext_ref, state_ref, *, L):
    nb, _, W = q_ref.shape
    Cn, Dh, H = CHUNK, HEAD_DIM, DN_HEADS
    nc = L // Cn

    @pl.when(pl.program_id(0) == 0)
    def _():
        ext_ref[:, 0:_DN_HALO, :] = jnp.zeros((nb, _DN_HALO, 3 * W), F32)
        state_ref[...] = jnp.zeros(state_ref.shape, F32)

    for b in range(nb):
        ext_ref[b, _DN_HALO:_DN_HALO + L, 0:W] = q_ref[b]
        ext_ref[b, _DN_HALO:_DN_HALO + L, W:2 * W] = k_ref[b]
        ext_ref[b, _DN_HALO:_DN_HALO + L, 2 * W:3 * W] = v_ref[b]

    i6 = lax.broadcasted_iota(I32, (Cn, Cn), 0)
    j6 = lax.broadcasted_iota(I32, (Cn, Cn), 1)
    causal = i6 >= j6
    strict = i6 > j6
    eye = jnp.where(i6 == j6, 1.0, 0.0).astype(F32)
    i7 = lax.broadcasted_iota(I32, (Cn, 2 * Cn), 0)
    j7 = lax.broadcasted_iota(I32, (Cn, 2 * Cn), 1)
    after = jnp.logical_or(i7 > j7, j7 >= Cn)
    i9 = lax.broadcasted_iota(I32, (Cn, 3 * Cn), 0)
    j9 = lax.broadcasted_iota(I32, (Cn, 3 * Cn), 1)
    tri3 = jnp.where((j9 & (Cn - 1)) <= i9, 1.0, 0.0).astype(BF16)

    def level_mask(s):
        same = ((i6 ^ j6) >> (s.bit_length())) == 0
        return jnp.logical_and(same, jnp.logical_and((i6 & s) != 0, (j6 & s) == 0))

    first = _DN_HALO - (SHORT_CONV - 1)

    def conv_act(b, r0, c0):
        acc = None
        for j in range(SHORT_CONV):
            t = scw_ref[j:j + 1, c0:c0 + Dh] * ext_ref[b, first + j + r0:first + j + r0 + Cn, c0:c0 + Dh]
            acc = t if acc is None else acc + t
        return _silu(acc)

    tiles = [(b, c) for b in range(nb) for c in range(nc)]
    units = [(b, c, h) for (b, c) in tiles for h in range(H)]

    beta_all, g_all = {}, {}
    for (b, c) in tiles:
        gt = gt_ref[b, c * Cn:(c + 1) * Cn, :]
        beta_all[b, c] = _sigmoid(gt)
        sp_in = gt + dtb_ref[...]
        softplus = jnp.maximum(sp_in, 0.0) + jnp.log1p(jnp.exp(-jnp.abs(sp_in)))
        g_all[b, c] = -jnp.exp(alog_ref[...]) * softplus

    dm = {}
    for (b, c, h) in units:
        g_b = jnp.broadcast_to(g_all[b, c][:, H + h:H + h + 1], (Cn, 2 * Cn))
        hi, mid, lo = _split3(jnp.where(after, g_b, 0.0))
        dm[b, c, h] = _dot(tri3, jnp.concatenate([hi, mid, lo], axis=0))

    qn, kn, kb, vb = {}, {}, {}, {}
    for (b, c, h) in units:
        r0, c0 = c * Cn, h * Dh
        qh = conv_act(b, r0, c0)
        kh = conv_act(b, r0, W + c0)
        vh = conv_act(b, r0, 2 * W + c0)
        qn[b, c, h] = qh * lax.rsqrt(jnp.sum(qh * qh, -1, keepdims=True) + L2_EPS) * (Dh ** -0.5)
        kh = kh * lax.rsqrt(jnp.sum(kh * kh, -1, keepdims=True) + L2_EPS)
        beta_b = jnp.broadcast_to(beta_all[b, c][:, h:h + 1], (Cn, Dh))
        kn[b, c, h] = kh
        kb[b, c, h] = kh * beta_b
        vb[b, c, h] = vh * beta_b

    decay, G = {}, {}
    for u in units:
        decay[u] = jnp.where(causal, jnp.exp(dm[u][:, :Cn]), 0.0)
        G[u] = jnp.broadcast_to(dm[u][:, Cn:Cn + 1], (Cn, Dh))

    lower, attn = {}, {}
    for u in units:
        kh_bf = kn[u].astype(BF16)
        lower[u] = jnp.where(strict, _dot_nt(kb[u].astype(BF16), kh_bf) * decay[u], 0.0)
        attn[u] = (_dot_nt(qn[u].astype(BF16), kh_bf) * decay[u]).astype(BF16)

    m1 = level_mask(1)
    X = {u: eye - jnp.where(m1, lower[u], 0.0) for u in units}
    s = 2
    while s < Cn:
        ms = level_mask(s)
        Xb = {u: X[u].astype(BF16) for u in units}
        XC = {u: _dot(Xb[u], jnp.where(ms, lower[u], 0.0).astype(BF16)).astype(BF16) for u in units}
        X = {u: X[u] - _dot(XC[u], Xb[u]) for u in units}
        s *= 2

    uu, wq, kd, sdec = {}, {}, {}, {}
    for u in units:
        exp_g = jnp.exp(G[u])
        sol = _dot(X[u].astype(BF16), jnp.concatenate([vb[u], kb[u] * exp_g], axis=1).astype(BF16))
        uu[u] = sol[:, :Dh]
        wq[u] = jnp.concatenate([sol[:, Dh:], qn[u] * exp_g], axis=0).astype(BF16)
        G_last = jnp.broadcast_to(G[u][Cn - 1:Cn, :], (Cn, Dh))
        kd[u] = (kn[u] * jnp.exp(G_last - G[u])).astype(BF16)
        e_last = jnp.exp(G_last)
        sdec[u] = jnp.concatenate([e_last, e_last], axis=0)

    states = {(b, h): state_ref[b * H + h] for b in range(nb) for h in range(H)}
    for c in range(nc):
        cur = [(b, c, h) for b in range(nb) for h in range(H)]
        r = {u: _dot(wq[u], states[u[0], u[2]].astype(BF16)) for u in cur}
        vnb = {u: (uu[u] - r[u][:Cn]).astype(BF16) for u in cur}
        for u in cur:
            b, _, h = u
            o = r[u][Cn:] + _dot(attn[u], vnb[u])
            states[b, h] = states[b, h] * sdec[u] + _dot_tn(kd[u], vnb[u])
            on = o * lax.rsqrt(jnp.mean(o * o, -1, keepdims=True) + RMS_EPS) * ong_ref[...]
            zh = z_ref[b, c * Cn:(c + 1) * Cn, h * Dh:(h + 1) * Dh]
            o_ref[b, c * Cn:(c + 1) * Cn, h * Dh:(h + 1) * Dh] = (on * _silu(zh)).astype(o_ref.dtype)

    for b in range(nb):
        for h in range(H):
            state_ref[b * H + h] = states[b, h]
        ext_ref[b, 0:_DN_HALO, :] = ext_ref[b, L:L + _DN_HALO, :]


def _deltanet(h, B, S, col0, sc_w, a_log, dt_bias, on_g, L):
    W = DN_HEADS * HEAD_DIM
    L = min(L, S)
    cb = col0 // W
    gb = (col0 + 4 * W) // LANES
    h3 = h.reshape(B, S, h.shape[1])
    scw = jnp.pad(sc_w, ((0, SUBLANES - SHORT_CONV), (0, 0)))
    lane_pad = lambda v: jnp.pad(v.reshape(1, -1), ((0, 0), (DN_HEADS, LANES - 2 * DN_HEADS)))
    vec = pl.BlockSpec((1, LANES), lambda s: (0, 0))
    col = lambda j: pl.BlockSpec((B, L, W), lambda s: (0, s, cb + j))
    out = pl.pallas_call(
        functools.partial(_deltanet_kernel, L=L),
        grid=(S // L,),
        in_specs=[col(0), col(1), col(2), col(3),
                  pl.BlockSpec((B, L, LANES), lambda s: (0, s, gb)),
                  pl.BlockSpec((SUBLANES, 3 * W), lambda s: (0, 0)),
                  vec, vec, vec],
        out_specs=pl.BlockSpec((B, L, W), lambda s: (0, s, 0)),
        out_shape=jax.ShapeDtypeStruct((B, S, W), BF16),
        scratch_shapes=[pltpu.VMEM((B, L + _DN_HALO, 3 * W), F32),
                        pltpu.VMEM((B * DN_HEADS, HEAD_DIM, HEAD_DIM), F32)],
        compiler_params=_cparams(1),
    )(h3, h3, h3, h3, h3, scw, lane_pad(a_log), lane_pad(dt_bias), on_g.reshape(1, HEAD_DIM))
    return out.reshape(B * S, W)


def _outproj_kernel(x_ref, yc_ref, yd_ref, w_ref, g_ref, b_ref, *rest, n_experts):
    Cc = yc_ref.shape[1]
    m = _dot(yc_ref[...], w_ref[0:Cc, :]) + _dot(yd_ref[...], w_ref[Cc:, :])
    x1 = _layer_norm(DEEPNORM_ALPHA * x_ref[...] + m, g_ref[...], b_ref[...])
    if not n_experts:
        (o_ref,) = rest
        o_ref[...] = x1
        return
    rw_ref, o_ref, meta_ref, gates_ref, cnt_ref, run_ref = rest
    o_ref[...] = x1
    tm = x1.shape[0]

    @pl.when(pl.program_id(0) == 0)
    def _():
        run_ref[...] = jnp.zeros(run_ref.shape, F32)

    xh, xm, _ = _split3(x1)
    wh, wm, _ = _split3(rw_ref[...])
    logits = _dot(xh, wh) + (_dot(xh, wm) + _dot(xm, wh))
    lane = lax.broadcasted_iota(I32, (tm, LANES), 1)
    lg = jnp.where(lane < n_experts, logits, -jnp.inf)
    v1 = jnp.max(lg, -1, keepdims=True)
    i1 = jnp.min(jnp.where(lg == v1, lane, LANES), -1, keepdims=True)
    lg2 = jnp.where(lane == i1, -jnp.inf, lg)
    v2 = jnp.max(lg2, -1, keepdims=True)
    i2 = jnp.min(jnp.where(lg2 == v2, lane, LANES), -1, keepdims=True)
    e2 = jnp.exp(v2 - v1)
    g1 = 1.0 / (1.0 + e2)
    g2 = e2 / (1.0 + e2)

    onehot = jnp.where(jnp.logical_or(lane == i1, lane == i2), 1.0, 0.0)
    ti = lax.broadcasted_iota(I32, (tm, tm), 0)
    tj = lax.broadcasted_iota(I32, (tm, tm), 1)
    before = jnp.where(tj < ti, 1.0, 0.0).astype(BF16)
    seen = _dot(before, onehot.astype(BF16)) + run_ref[0:1, :]
    r1 = jnp.sum(jnp.where(lane == i1, seen, 0.0), -1, keepdims=True).astype(I32)
    r2 = jnp.sum(jnp.where(lane == i2, seen, 0.0), -1, keepdims=True).astype(I32)
    run_ref[...] = run_ref[...] + jnp.sum(onehot, 0, keepdims=True)
    cnt_ref[...] = run_ref[...]

    meta_ref[...] = jnp.where(lane == 0, i1, jnp.where(lane == 1, i2, jnp.where(lane == 2, r1, jnp.where(lane == 3, r2, 0))))
    gates_ref[...] = jnp.where(lane == 0, g1, jnp.where(lane == 1, g2, 0.0))


def _out_proj(xt, yc, yd, w, g, b, router_w=None):
    T, D = xt.shape
    Cc, Cd = yc.shape[1], yd.shape[1]
    tm = min(512, T)
    vec = pl.BlockSpec((1, D), lambda i: (0, 0))
    rows = lambda n: pl.BlockSpec((tm, n), lambda i: (i, 0))
    in_specs = [rows(D), rows(Cc), rows(Cd), pl.BlockSpec((Cc + Cd, D), lambda i: (0, 0)), vec, vec]
    args = [xt, yc, yd, w, g.reshape(1, D), b.reshape(1, D)]
    if router_w is None:
        return pl.pallas_call(
            functools.partial(_outproj_kernel, n_experts=0),
            grid=(T // tm,), in_specs=in_specs, out_specs=rows(D),
            out_shape=jax.ShapeDtypeStruct((T, D), F32), compiler_params=_cparams(1),
        )(*args)
    E = router_w.shape[1]
    rw = jnp.pad(router_w, ((0, 0), (0, LANES - E)))
    return pl.pallas_call(
        functools.partial(_outproj_kernel, n_experts=E),
        grid=(T // tm,),
        in_specs=in_specs + [pl.BlockSpec((D, LANES), lambda i: (0, 0))],
        out_specs=[rows(D), rows(LANES), rows(LANES), pl.BlockSpec((SUBLANES, LANES), lambda i: (0, 0))],
        out_shape=[jax.ShapeDtypeStruct((T, D), F32), jax.ShapeDtypeStruct((T, LANES), I32),
                   jax.ShapeDtypeStruct((T, LANES), F32), jax.ShapeDtypeStruct((SUBLANES, LANES), F32)],
        scratch_shapes=[pltpu.VMEM((SUBLANES, LANES), F32)],
        compiler_params=_cparams(1),
    )(*args, rw)


_FF_CHUNK = 256


def _ffn_dense_kernel(x_ref, wg_ref, wu_ref, wd_ref, g_ref, b_ref, o_ref, xb_ref, *, tf):
    F = wg_ref.shape[1]
    xb_ref[...] = x_ref[...].astype(BF16)
    for f0 in range(0, F, tf):
        xb = xb_ref[...]
        gate = _dot(xb, wg_ref[:, f0:f0 + tf])
        up = _dot(xb, wu_ref[:, f0:f0 + tf])
        part = _dot((_silu(gate) * up).astype(BF16), wd_ref[f0:f0 + tf, :])
        if f0 == 0:
            o_ref[...] = part
        else:
            o_ref[...] = o_ref[...] + part
    o_ref[...] = _layer_norm(DEEPNORM_ALPHA * x_ref[...] + o_ref[...], g_ref[...], b_ref[...])


def _ff_tile(F, target):
    best = LANES
    for t in range(LANES, min(F, target) + 1, LANES):
        if F % t == 0:
            best = t
    return best


def _ffn_dense(xt, wg, wu, wd, g, b):
    T, D = xt.shape
    F = wg.shape[1]
    tm = min(1024, T)
    tf = _ff_tile(F, _FF_CHUNK)
    vec = pl.BlockSpec((1, D), lambda i: (0, 0))
    return pl.pallas_call(
        functools.partial(_ffn_dense_kernel, tf=tf),
        grid=(T // tm,),
        in_specs=[pl.BlockSpec((tm, D), lambda i: (i, 0)),
                  _resident((D, F), lambda i: (0, 0)),
                  _resident((D, F), lambda i: (0, 0)),
                  _resident((F, D), lambda i: (0, 0)),
                  vec, vec],
        out_specs=pl.BlockSpec((tm, D), lambda i: (i, 0)),
        out_shape=jax.ShapeDtypeStruct((T, D), F32),
        scratch_shapes=[pltpu.VMEM((tm, D), BF16)],
        compiler_params=_cparams(1),
    )(xt, wg, wu, wd, g.reshape(1, D), b.reshape(1, D))


_GATHER_UNROLL = 8


def _row_copy(src_hbm, idx_ref, k, dst_ref, r, sem):
    return pltpu.make_async_copy(src_hbm.at[pl.ds(idx_ref[0, 0, k], 1), :], dst_ref.at[pl.ds(r, 1), :], sem)


def _gather_rows(src_hbm, idx_ref, n_rows, stride, offset, dst_ref, sem):
    def issue(r, carry):
        _row_copy(src_hbm, idx_ref, stride * r + offset, dst_ref, r, sem).start()
        return carry

    lax.fori_loop(0, n_rows, issue, 0, unroll=_GATHER_UNROLL)


def _gather_rows_inline(src_hbm, idx_ref, rows, stride, offset, dst_ref, sem):
    for r in rows:
        _row_copy(src_hbm, idx_ref, stride * r + offset, dst_ref, r, sem).start(priority=r % 2)


def _wait_rows(src_hbm, dst_ref, sem):
    pltpu.make_async_copy(src_hbm.at[pl.ds(0, dst_ref.shape[0]), :], dst_ref, sem).wait()


def _moe_kernel(te_ref, nv_ref, src_ref, nxt_ref, x_hbm, wg_ref, wu_ref, wd_ref, o_ref, xs_ref, xb_ref, sems, *, tf):
    i = pl.program_id(0)
    tm = xs_ref.shape[1]
    F = wg_ref.shape[1]
    slot = i % 2
    nxt_buf, nxt_sem = xs_ref.at[1 - slot], sems.at[1 - slot]
    n_chunks = F // tf

    @pl.when(i == 0)
    def _():
        _gather_rows(x_hbm, src_ref, tm, 1, 0, xs_ref.at[0], sems.at[0])

    _wait_rows(x_hbm, xs_ref.at[slot], sems.at[slot])

    @pl.when(i < nv_ref[0])
    def _():
        xb_ref[...] = xs_ref[slot].astype(BF16)
        for c in range(n_chunks):
            f0 = c * tf
            xb = xb_ref[...]
            gate = _dot(xb, wg_ref[:, f0:f0 + tf])
            up = _dot(xb, wu_ref[:, f0:f0 + tf])
            part = _dot((_silu(gate) * up).astype(BF16), wd_ref[f0:f0 + tf, :])
            if c == 0:
                o_ref[...] = part
            else:
                o_ref[...] = o_ref[...] + part
            rows = range((c * tm) // n_chunks, ((c + 1) * tm) // n_chunks)
            _gather_rows_inline(x_hbm, nxt_ref, rows, 1, 0, nxt_buf, nxt_sem)

    @pl.when(i >= nv_ref[0])
    def _():
        o_ref[...] = jnp.zeros(o_ref.shape, o_ref.dtype)
        _gather_rows(x_hbm, nxt_ref, tm, 1, 0, nxt_buf, nxt_sem)

    @pl.when(i == pl.num_programs(0) - 1)
    def _():
        _wait_rows(x_hbm, nxt_buf, nxt_sem)


def _moe_experts(x1, src, tile_expert, n_valid, wg, wu, wd, tm):
    T, D = x1.shape
    F = wg.shape[2]
    n_tiles = src.shape[0]
    tf = _ff_tile(F, _FF_CHUNK)
    idx_spec = lambda step: pl.BlockSpec((1, 1, tm), lambda i, te, nv: (jnp.minimum(i + step, n_tiles - 1), 0, 0),
                                         memory_space=pltpu.SMEM)
    expert_w = lambda shape: pl.BlockSpec((None,) + shape, lambda i, te, nv: (te[i], 0, 0),
                                          pipeline_mode=pl.Buffered(1))
    grid_spec = pltpu.PrefetchScalarGridSpec(
        num_scalar_prefetch=2,
        grid=(n_tiles,),
        in_specs=[idx_spec(0), idx_spec(1),
                  pl.BlockSpec(memory_space=pl.ANY),
                  expert_w((D, F)), expert_w((D, F)), expert_w((F, D))],
        out_specs=pl.BlockSpec((tm, D), lambda i, te, nv: (i, 0)),
        scratch_shapes=[pltpu.VMEM((2, tm, D), F32), pltpu.VMEM((tm, D), BF16),
                        pltpu.SemaphoreType.DMA((2,))],
    )
    return pl.pallas_call(
        functools.partial(_moe_kernel, tf=tf), grid_spec=grid_spec,
        out_shape=jax.ShapeDtypeStruct((n_tiles * tm, D), F32),
        compiler_params=_cparams(1),
    )(tile_expert, n_valid, src, src, x1, wg, wu, wd)


def _combine_kernel(pos_ref, nxt_ref, x_ref, gates_ref, ys_hbm, g_ref, b_ref, o_ref, a_ref, b2_ref, sems):
    i = pl.program_id(0)
    tc = x_ref.shape[0]
    slot = i % 2

    @pl.when(i == 0)
    def _():
        _gather_rows(ys_hbm, pos_ref, tc, TOP_K, 0, a_ref.at[0], sems.at[0, 0])
        _gather_rows(ys_hbm, pos_ref, tc, TOP_K, 1, b2_ref.at[0], sems.at[1, 0])

    _wait_rows(ys_hbm, a_ref.at[slot], sems.at[0, slot])
    _wait_rows(ys_hbm, b2_ref.at[slot], sems.at[1, slot])
    _gather_rows_inline(ys_hbm, nxt_ref, range(tc), TOP_K, 0, a_ref.at[1 - slot], sems.at[0, 1 - slot])
    _gather_rows_inline(ys_hbm, nxt_ref, range(tc), TOP_K, 1, b2_ref.at[1 - slot], sems.at[1, 1 - slot])
    gts = gates_ref[...]
    f = gts[:, 0:1] * a_ref[slot] + gts[:, 1:2] * b2_ref[slot]
    o_ref[...] = _layer_norm(DEEPNORM_ALPHA * x_ref[...] + f, g_ref[...], b_ref[...])

    @pl.when(i == pl.num_programs(0) - 1)
    def _():
        _wait_rows(ys_hbm, a_ref.at[1 - slot], sems.at[0, 1 - slot])
        _wait_rows(ys_hbm, b2_ref.at[1 - slot], sems.at[1, 1 - slot])


def _moe_combine(x1, pos, gates, ys, g, b):
    T, D = x1.shape
    tc = min(256, T)
    n = T // tc
    vec = pl.BlockSpec((1, D), lambda i: (0, 0))
    idx_spec = lambda step: pl.BlockSpec((1, 1, TOP_K * tc), lambda i: (jnp.minimum(i + step, n - 1), 0, 0),
                                         memory_space=pltpu.SMEM)
    pos3 = pos.reshape(n, 1, TOP_K * tc)
    return pl.pallas_call(
        _combine_kernel,
        grid=(n,),
        in_specs=[idx_spec(0), idx_spec(1),
                  pl.BlockSpec((tc, D), lambda i: (i, 0)),
                  pl.BlockSpec((tc, LANES), lambda i: (i, 0)),
                  pl.BlockSpec(memory_space=pl.ANY),
                  vec, vec],
        out_specs=pl.BlockSpec((tc, D), lambda i: (i, 0)),
        out_shape=jax.ShapeDtypeStruct((T, D), F32),
        scratch_shapes=[pltpu.VMEM((2, tc, D), F32), pltpu.VMEM((2, tc, D), F32),
                        pltpu.SemaphoreType.DMA((2, 2))],
        compiler_params=_cparams(1),
    )(pos3, pos3, x1, gates, ys, g.reshape(1, D), b.reshape(1, D))


def _moe_ffn(x1, meta, gates, counts, wg, wu, wd, g, b):
    T, D = x1.shape
    E = wg.shape[0]
    tm = min(1024, T)
    n_tiles = (TOP_K * T) // tm + E - 1
    cnt = counts[0, :E].astype(I32)
    tiles_e = (cnt + tm - 1) // tm
    tile_end = jnp.cumsum(tiles_e)
    row_off = (tile_end - tiles_e) * tm
    idx, rank = meta[:, 0:TOP_K], meta[:, TOP_K:2 * TOP_K]
    pos = row_off[idx] + rank
    tok = jnp.broadcast_to(jnp.arange(T, dtype=I32)[:, None], (T, TOP_K))
    src = jnp.zeros((n_tiles * tm,), I32).at[pos.reshape(-1)].set(tok.reshape(-1), unique_indices=True)
    tile_ids = jnp.arange(n_tiles, dtype=I32)
    tile_expert = jnp.minimum(jnp.sum((tile_ids[:, None] >= tile_end[None, :]).astype(I32), axis=1), E - 1)
    n_valid = tile_end[E - 1:E].astype(I32)
    ys = _moe_experts(x1, src.reshape(n_tiles, 1, tm), tile_expert, n_valid, wg, wu, wd, tm)
    return _moe_combine(x1, pos, gates, ys, g, b)


def kernel(x, w_in, conv_dw_w, conv_dw_b, conv_ln_g, conv_ln_b, short_conv_w, a_log, dt_bias, out_norm_g, w_out, ln_mix_g, ln_mix_b, ffn_w_gate, ffn_w_up, ffn_w_down, router_w, moe_w_gate, moe_w_up, moe_w_down, ln_ffn_g, ln_ffn_b):
    B, S, D = x.shape
    T = B * S
    xt = x.reshape(T, D)
    Cc = conv_dw_w.shape[2]
    in_cols = w_in.shape[2]
    in_cols_pad = -(-in_cols // LANES) * LANES
    for layer in range(w_in.shape[0]):
        w_in_p = jnp.pad(w_in[layer], ((0, 0), (0, in_cols_pad - in_cols))).astype(BF16)
        h = _in_proj(xt, w_in_p)
        yc = _conv_mixer(h, B, S, conv_dw_w[layer], conv_dw_b[layer], conv_ln_g[layer], conv_ln_b[layer])
        yd = _deltanet(h, B, S, 2 * Cc, short_conv_w[layer], a_log[layer], dt_bias[layer], out_norm_g[layer], L=256)
        j = layer // 2
        w_o = w_out[layer].astype(BF16)
        if layer % 2 == 0:
            x1 = _out_proj(xt, yc, yd, w_o, ln_mix_g[layer], ln_mix_b[layer])
            xt = _ffn_dense(x1, ffn_w_gate[j].astype(BF16), ffn_w_up[j].astype(BF16), ffn_w_down[j].astype(BF16),
                            ln_ffn_g[layer], ln_ffn_b[layer])
        else:
            x1, meta, gates, counts = _out_proj(xt, yc, yd, w_o, ln_mix_g[layer], ln_mix_b[layer], router_w[j])
            xt = _moe_ffn(x1, meta, gates, counts, moe_w_gate[j].astype(BF16), moe_w_up[j].astype(BF16),
                          moe_w_down[j].astype(BF16), ln_ffn_g[layer], ln_ffn_b[layer])
    return xt.reshape(B, S, D)
```
